```python
import math
import jax, jax.numpy as jnp
from jax import lax
import numpy as np

D_MODEL = 1024
BATCH = 8
SEQ = 4096
DEPTH = 2

N_MIXERS = 2
N_POOL_LAYERS = (DEPTH + 1) // 2
N_DN_LAYERS = DEPTH // 2

RMS_EPS = 1e-6

POOL_WINDOWS = (2, 4, 8, 16)
POOL_GROUPS = len(POOL_WINDOWS)
POOL_GROUP_DIM = D_MODEL // POOL_GROUPS

DN_HEAD_DIM = 128
DN_HEADS = D_MODEL // DN_HEAD_DIM
DN_KEY_DIM = DN_HEADS * DN_HEAD_DIM
DN_CONV = 4
DN_CHUNK = 64
DN_IN_DIM = 4 * DN_KEY_DIM + 2 * DN_HEADS

DENSE_FF = 2816
N_EXPERTS = 8
TOP_K = 2
EXPERT_FF = 3584

kernel_name = "hybrid_pool_gdn_moe_trunk"


def rmsnorm(x, g):
    x32 = x.astype(jnp.float32)
    y = x32 * lax.rsqrt(jnp.mean(x32 * x32, axis=-1, keepdims=True) + RMS_EPS)
    return (y * g.astype(jnp.float32)).astype(x.dtype)


def l2norm(x):
    return x * lax.rsqrt(jnp.sum(x * x, axis=-1, keepdims=True) + RMS_EPS)


def swiglu(x, w_gate, w_up, w_down):
    return (jax.nn.silu(x @ w_gate) * (x @ w_up)) @ w_down


def pool_mixer(h, w_in, w_group, scale):
    b, s, d = h.shape
    u = (h @ w_in).reshape(b, s, POOL_GROUPS, POOL_GROUP_DIM).astype(jnp.float32)
    csum = jnp.cumsum(u, axis=1)
    pos = jnp.arange(1, s + 1, dtype=jnp.float32)[:, None]
    outs = []
    for gi, w in enumerate(POOL_WINDOWS):
        c = csum[:, :, gi]
        c_prev = jnp.pad(c, ((0, 0), (w, 0), (0, 0)))[:, :s]
        cnt = jnp.minimum(pos, float(w))
        outs.append((c - c_prev) / cnt - u[:, :, gi])
    m = jnp.stack(outs, axis=2).astype(h.dtype)
    y = jnp.einsum('bsgc,gce->bsge', m, w_group).reshape(b, s, d)
    return y * scale


def causal_depthwise_conv(x, w):
    k, c = w.shape
    return lax.conv_general_dilated(
        x, w[:, None, :], window_strides=(1,), padding=[(k - 1, 0)],
        dimension_numbers=('NWC', 'WIO', 'NWC'), feature_group_count=c)


def gated_delta_rule_chunked(q, k, v, beta, g):
    b, s, h, dk = q.shape
    dv = v.shape[-1]
    c = DN_CHUNK
    n = s // c

    def to_chunks(t):
        t = t.reshape((b, n, c, h) + t.shape[3:])
        return jnp.moveaxis(t, 3, 1)

    q, k, v, beta, g = (to_chunks(t) for t in (q, k, v, beta, g))
    gc = jnp.cumsum(g, axis=-1)
    causal = jnp.tril(jnp.ones((c, c), dtype=bool))
    strict = jnp.tril(jnp.ones((c, c), dtype=bool), -1)
    decay = jnp.exp(jnp.where(causal, gc[..., :, None] - gc[..., None, :], -jnp.inf))

    k_beta = k * beta[..., None]
    v_beta = v * beta[..., None]
    a = jnp.einsum('bhnid,bhnjd->bhnij', k_beta, k) * decay
    a = jnp.where(strict, a, 0.0) + jnp.eye(c, dtype=a.dtype)
    u = lax.linalg.triangular_solve(a, v_beta, left_side=True, lower=True, unit_diagonal=True)
    w = lax.linalg.triangular_solve(a, k_beta * jnp.exp(gc)[..., None],
                                    left_side=True, lower=True, unit_diagonal=True)
    qk = jnp.einsum('bhnid,bhnjd->bhnij', q, k) * decay

    xs = tuple(jnp.moveaxis(t, 2, 0) for t in (q, k, u, w, qk, gc))

    def step(state, inp):
        q_c, k_c, u_c, w_c, qk_c, gc_c = inp
        v_new = u_c - jnp.einsum('bhck,bhkv->bhcv', w_c, state)
        o = (jnp.einsum('bhck,bhkv->bhcv', q_c * jnp.exp(gc_c)[..., None], state)
             + jnp.einsum('bhij,bhjv->bhiv', qk_c, v_new))
        g_last = gc_c[..., -1]
        k_dec = k_c * jnp.exp(g_last[..., None] - gc_c)[..., None]
        state = state * jnp.exp(g_last)[..., None, None] + jnp.einsum('bhck,bhcv->bhkv', k_dec, v_new)
        return state, o

    s0 = jnp.zeros((b, h, dk, dv), jnp.float32)
    _, o = lax.scan(step, s0, xs)
    return jnp.transpose(o, (1, 0, 3, 2, 4)).reshape(b, s, h, dv)


def deltanet_mixer(h, w_in, conv_w, a_log, dt_bias, norm_g, w_out):
    b, s, _ = h.shape
    kd = DN_KEY_DIM
    proj = h @ w_in
    qkv = proj[..., :3 * kd]
    z = proj[..., 3 * kd:4 * kd]
    b_logit = proj[..., 4 * kd:4 * kd + DN_HEADS].astype(jnp.float32)
    a_in = proj[..., 4 * kd + DN_HEADS:].astype(jnp.float32)
    qkv = jax.nn.silu(causal_depthwise_conv(qkv, conv_w)).astype(jnp.float32)
    q, k, v = jnp.split(qkv, 3, axis=-1)
    q = l2norm(q.reshape(b, s, DN_HEADS, DN_HEAD_DIM)) * (DN_HEAD_DIM ** -0.5)
    k = l2norm(k.reshape(b, s, DN_HEADS, DN_HEAD_DIM))
    v = v.reshape(b, s, DN_HEADS, DN_HEAD_DIM)
    beta = jax.nn.sigmoid(b_logit)
    g = -jnp.exp(a_log.astype(jnp.float32)) * jax.nn.softplus(a_in + dt_bias.astype(jnp.float32))
    o = gated_delta_rule_chunked(q, k, v, beta, g)
    o = o * lax.rsqrt(jnp.mean(o * o, axis=-1, keepdims=True) + RMS_EPS) * norm_g.astype(jnp.float32)
    o = o * jax.nn.silu(z.astype(jnp.float32)).reshape(b, s, DN_HEADS, DN_HEAD_DIM)
    return o.reshape(b, s, kd).astype(h.dtype) @ w_out


def moe_ffn(h, router_w, router_b, w_gate, w_up, w_down):
    b, s, d = h.shape
    t = h.reshape(b * s, d)
    logits = (t @ router_w).astype(jnp.float32) + router_b.astype(jnp.float32)
    top_vals, top_idx = lax.top_k(logits, TOP_K)
    top_w = jax.nn.softmax(top_vals, axis=-1)
    gates = jnp.sum(jax.nn.one_hot(top_idx, N_EXPERTS, dtype=jnp.float32) * top_w[..., None], axis=1)
    out = jnp.zeros((b * s, d), jnp.float32)
    for e in range(N_EXPERTS):
        y = swiglu(t, w_gate[e], w_up[e], w_down[e]).astype(jnp.float32)
        out = out + gates[:, e:e + 1] * y
    return out.astype(h.dtype).reshape(b, s, d)


def setup_inputs(seed: int = 0) -> dict:
    key = jax.random.key(seed)
    ks = jax.random.split(key, 24)
    f32 = jnp.float32
    D = D_MODEL
    NP, ND = N_POOL_LAYERS, N_DN_LAYERS

    def nrm(k, shape, fan_in):
        return jax.random.normal(k, shape, f32) * (fan_in ** -0.5)

    def gain(k, shape):
        return 1.0 + 0.05 * jax.random.normal(k, shape, f32)

    dt = jnp.exp(jax.random.uniform(ks[10], (ND, DN_HEADS), f32,
                                    minval=math.log(1e-3), maxval=math.log(1e-1)))
    return {
        "x": jax.random.normal(ks[0], (BATCH, SEQ, D), f32),
        "norm_mix_g": gain(ks[1], (DEPTH, D)),
        "norm_ffn_g": gain(ks[2], (DEPTH, D)),
        "pool_w_in": nrm(ks[3], (NP, D, D), D),
        "pool_w_group": nrm(ks[4], (NP, POOL_GROUPS, POOL_GROUP_DIM, POOL_GROUP_DIM), POOL_GROUP_DIM),
        "pool_scale": gain(ks[5], (NP, D)),
        "dn_w_in": nrm(ks[6], (ND, D, DN_IN_DIM), D),
        "dn_conv_w": nrm(ks[7], (ND, DN_CONV, 3 * DN_KEY_DIM), DN_CONV),
        "dn_a_log": jnp.log(jax.random.uniform(ks[8], (ND, DN_HEADS), f32, minval=1.0, maxval=16.0)),
        "dn_dt_bias": dt + jnp.log(-jnp.expm1(-dt)),
        "dn_norm_g": gain(ks[9], (ND, DN_HEAD_DIM)),
        "dn_w_out": nrm(ks[11], (ND, DN_KEY_DIM, D), DN_KEY_DIM),
        "ffn_w_gate": nrm(ks[12], (NP, D, DENSE_FF), D),
        "ffn_w_up": nrm(ks[13], (NP, D, DENSE_FF), D),
        "ffn_w_down": nrm(ks[14], (NP, DENSE_FF, D), DENSE_FF),
        "moe_router_w": nrm(ks[15], (ND, D, N_EXPERTS), D),
        "moe_router_b": 0.01 * jax.random.normal(ks[16], (ND, N_EXPERTS), f32),
        "moe_w_gate": nrm(ks[17], (ND, N_EXPERTS, D, EXPERT_FF), D),
        "moe_w_up": nrm(ks[18], (ND, N_EXPERTS, D, EXPERT_FF), D),
        "moe_w_down": nrm(ks[19], (ND, N_EXPERTS, EXPERT_FF, D), EXPERT_FF),
        "final_norm_g": gain(ks[20], (D,)),
    }


def reference(x, norm_mix_g, norm_ffn_g, pool_w_in, pool_w_group, pool_scale,
              dn_w_in, dn_conv_w, dn_a_log, dn_dt_bias, dn_norm_g, dn_w_out,
              ffn_w_gate, ffn_w_up, ffn_w_down,
              moe_router_w, moe_router_b, moe_w_gate, moe_w_up, moe_w_down,
              final_norm_g):
    h = x
    for i in range(DEPTH):
        j = i // N_MIXERS
        hn = rmsnorm(h, norm_mix_g[i])
        if i % N_MIXERS == 0:
            h = h + pool_mixer(hn, pool_w_in[j], pool_w_group[j], pool_scale[j])
        else:
            h = h + deltanet_mixer(hn, dn_w_in[j], dn_conv_w[j], dn_a_log[j], dn_dt_bias[j],
                                   dn_norm_g[j], dn_w_out[j])
        hn = rmsnorm(h, norm_ffn_g[i])
        if i % 2 == 0:
            h = h + swiglu(hn, ffn_w_gate[j], ffn_w_up[j], ffn_w_down[j])
        else:
            h = h + moe_ffn(hn, moe_router_w[j], moe_router_b[j], moe_w_gate[j], moe_w_up[j], moe_w_down[j])
    return rmsnorm(h, final_norm_g)
```

```python
import functools

import jax
import jax.numpy as jnp
from jax import lax
from jax.experimental import pallas as pl
from jax.experimental.pallas import tpu as pltpu

F32 = jnp.float32
BF16 = jnp.bfloat16

RMS_EPS = 1e-6
POOL_WINDOWS = (2, 4, 8, 16)
DN_HEAD_DIM = 128
DN_CHUNK = 64
TOP_K = 2

LANES = 128
SUBLANES = 8
VMEM_LIMIT_BYTES = 56 * 1024 * 1024

POOL_HALO = 16
CONV_HALO = SUBLANES


def _dot(a, b):
    return jnp.dot(a, b, preferred_element_type=F32)


def _dot_f32(a, b):
    return jnp.dot(a, b, preferred_element_type=F32, precision=lax.Precision.HIGHEST)


def _dot_nt(a, b):
    return lax.dot_general(a, b, (((1,), (1,)), ((), ())), preferred_element_type=F32)


def _dot_tn(a, b):
    return lax.dot_general(a, b, (((0,), (0,)), ((), ())), preferred_element_type=F32)


def _rms(x, g):
    return x * lax.rsqrt(jnp.mean(x * x, axis=-1, keepdims=True) + RMS_EPS) * g


def _silu(x):
    return x * jax.nn.sigmoid(x)


def _pick_tile(n, target, quantum):
    if n <= target:
        return n
    best = None
    for t in range(quantum, target + 1, quantum):
        if n % t == 0:
            best = t
    assert best is not None, (n, target, quantum)
    return best


def _params(semantics):
    return pltpu.CompilerParams(dimension_semantics=semantics, vmem_limit_bytes=VMEM_LIMIT_BYTES)


def _resident(shape):
    nd = len(shape)
    return pl.BlockSpec(shape, lambda *_: (0,) * nd, pipeline_mode=pl.Buffered(1))


def _pool_kernel(x_ref, xh_ref, g_ref, win_ref, wgrp_ref, sc_ref, o_ref, ext_ref, *, tiles_per_seq):
    tm = x_ref.shape[0]
    j = pl.program_id(0) % tiles_per_seq
    x = x_ref[...]
    g = g_ref[...]
    win = win_ref[...]
    u = _dot(_rms(x, g).astype(BF16), win)
    uh = _dot(_rms(xh_ref[...], g).astype(BF16), win)
    ext_ref[0:POOL_HALO, :] = jnp.where(j == 0, 0.0, uh)
    ext_ref[POOL_HALO:, :] = u
    pos = (j * tm + lax.broadcasted_iota(jnp.int32, (tm, 1), 0) + 1).astype(F32)
    c = u.shape[1] // len(POOL_WINDOWS)
    for gi, w in enumerate(POOL_WINDOWS):
        sl = slice(gi * c, (gi + 1) * c)
        s = ext_ref[:, sl]
        sh = 1
        while sh < w:
            s = s + pltpu.roll(s, sh, 0)
            sh *= 2
        m = s[POOL_HALO:] / jnp.minimum(pos, float(w)) - u[:, sl]
        y = _dot(m.astype(BF16), wgrp_ref[gi])
        o_ref[:, sl] = x[:, sl] + y * sc_ref[:, sl]


def _pool_mixer(x2d, seq, g, w_in, w_group, scale):
    t, d = x2d.shape
    assert all(w & (w - 1) == 0 and w <= POOL_HALO for w in POOL_WINDOWS)
    tm = _pick_tile(seq, 512, POOL_HALO)
    assert tm % POOL_HALO == 0
    halo_blocks = tm // POOL_HALO
    return pl.pallas_call(
        functools.partial(_pool_kernel, tiles_per_seq=seq // tm),
        grid=(t // tm,),
        in_specs=[
            pl.BlockSpec((tm, d), lambda i: (i, 0)),
            pl.BlockSpec((POOL_HALO, d), lambda i: (jnp.maximum(i * halo_blocks - 1, 0), 0)),
            _resident((1, d)),
            _resident(w_in.shape),
            _resident(w_group.shape),
            _resident((1, d)),
        ],
        out_specs=pl.BlockSpec((tm, d), lambda i: (i, 0)),
        out_shape=jax.ShapeDtypeStruct((t, d), F32),
        scratch_shapes=[pltpu.VMEM((tm + POOL_HALO, d), F32)],
        compiler_params=_params(("parallel",)),
        name="pool_mixer",
    )(x2d, x2d, g, w_in, w_group, scale)


def _ffn_kernel(*refs, gated, final_norm):
    refs = list(refs)
    h_ref, g_ref = refs[:2]
    del refs[:2]
    gates_ref = refs.pop(0) if gated else None
    wg_ref, wu_ref, wd_ref = refs[:3]
    del refs[:3]
    gf_ref = refs.pop(0) if final_norm else None
    o_ref, hn_ref, acc_ref = refs

    e = pl.program_id(1)
    f = pl.program_id(2)

    @pl.when((e == 0) & (f == 0))
    def _():
        hn_ref[...] = _rms(h_ref[...], g_ref[...]).astype(BF16)
        acc_ref[...] = jnp.zeros_like(acc_ref)

    hn = hn_ref[...]
    act = _silu(_dot(hn, wg_ref[0])) * _dot(hn, wu_ref[0])
    y = _dot(act.astype(BF16), wd_ref[0])
    if gated:
        gates = gates_ref[...]
        lane = lax.broadcasted_iota(jnp.int32, gates.shape, 1)
        y = y * jnp.sum(jnp.where(lane == e, gates, 0.0), axis=1, keepdims=True)
    acc_ref[...] += y

    @pl.when((e == pl.num_programs(1) - 1) & (f == pl.num_programs(2) - 1))
    def _():
        out = h_ref[...] + acc_ref[...]
        if final_norm:
            out = _rms(out, gf_ref[...])
        o_ref[...] = out


def _ffn(h, g, w_gate, w_up, w_down, *, gates=None, final_g=None, tm_target, tf_target):
    t, d = h.shape
    n_e, _, ff = w_gate.shape
    tm = _pick_tile(t, tm_target, SUBLANES)
    tf = _pick_tile(ff, tf_target, LANES)
    row = pl.BlockSpec((tm, d), lambda i, e, f: (i, 0))
    vec = pl.BlockSpec((1, d), lambda i, e, f: (0, 0))
    in_specs, args = [row, vec], [h, g]
    if gates is not None:
        in_specs.append(pl.BlockSpec((tm, gates.shape[1]), lambda i, e, f: (i, 0)))
        args.append(gates)
    in_specs += [
        pl.BlockSpec((1, d, tf), lambda i, e, f: (e, 0, f)),
        pl.BlockSpec((1, d, tf), lambda i, e, f: (e, 0, f)),
        pl.BlockSpec((1, tf, d), lambda i, e, f: (e, f, 0)),
    ]
    args += [w_gate, w_up, w_down]
    if final_g is not None:
        in_specs.append(vec)
        args.append(final_g)
    return pl.pallas_call(
        functools.partial(_ffn_kernel, gated=gates is not None, final_norm=final_g is not None),
        grid=(t // tm, n_e, ff // tf),
        in_specs=in_specs,
        out_specs=row,
        out_shape=jax.ShapeDtypeStruct((t, d), F32),
        scratch_shapes=[pltpu.VMEM((tm, d), BF16), pltpu.VMEM((tm, d), F32)],
        compiler_params=_params(("parallel", "arbitrary", "arbitrary")),
        name="moe_ffn" if gates is not None else "dense_ffn",
    )(*args)


def _dn_in_kernel(h_ref, hh_ref, g_ref, wqkv_ref, wz_ref, wsm_ref, cw_ref, alog_ref, dtb_ref,
                  q_ref, k_ref, v_ref, z_ref, bg_ref, ext_ref, *, tiles_per_seq):
    j = pl.program_id(0) % tiles_per_seq
    g = g_ref[...]
    hn = _rms(h_ref[...], g).astype(BF16)
    hnh = jnp.where(j == 0, 0.0, _rms(hh_ref[...], g)).astype(BF16)
    kd = q_ref.shape[1]
    heads = kd // DN_HEAD_DIM
    taps = cw_ref.shape[0]
    for part, out_ref in enumerate((q_ref, k_ref, v_ref)):
        cols = slice(part * kd, (part + 1) * kd)
        w = wqkv_ref[:, cols]
        ext_ref[0:CONV_HALO, :] = _dot(hnh, w)
        ext_ref[CONV_HALO:, :] = _dot(hn, w)
        p = ext_ref[...]
        cw = cw_ref[:, cols]
        y = p * cw[taps - 1:taps, :]
        for s in range(1, taps):
            y = y + pltpu.roll(p, s, 0) * cw[taps - 1 - s:taps - s, :]
        y = _silu(y[CONV_HALO:])
        if part == 2:
            out_ref[...] = y
            continue
        for h in range(heads):
            hs = slice(h * DN_HEAD_DIM, (h + 1) * DN_HEAD_DIM)
            yh = y[:, hs]
            yh = yh * lax.rsqrt(jnp.sum(yh * yh, axis=-1, keepdims=True) + RMS_EPS)
            if part == 0:
                yh = yh * (DN_HEAD_DIM ** -0.5)
            out_ref[:, hs] = yh
    z_ref[...] = _dot(hn, wz_ref[...])
    sm = _dot(hn, wsm_ref[...])
    xs = sm + dtb_ref[...]
    softplus = jnp.maximum(xs, 0.0) + jnp.log1p(jnp.exp(-jnp.abs(xs)))
    lane = lax.broadcasted_iota(jnp.int32, sm.shape, 1)
    bg_ref[...] = jnp.where(lane < heads, jax.nn.sigmoid(sm), -jnp.exp(alog_ref[...]) * softplus)


def _dn_in_proj(h, seq, g, w_qkv, w_z, w_small, conv_w, a_log_pad, dt_bias_pad):
    t, d = h.shape
    kd = w_z.shape[1]
    assert conv_w.shape[0] - 1 <= CONV_HALO
    tm = _pick_tile(seq, 512, CONV_HALO)
    halo_blocks = tm // CONV_HALO
    row = lambda n: pl.BlockSpec((tm, n), lambda i: (i, 0))
    return pl.pallas_call(
        functools.partial(_dn_in_kernel, tiles_per_seq=seq // tm),
        grid=(t // tm,),
        in_specs=[
            row(d),
            pl.BlockSpec((CONV_HALO, d), lambda i: (jnp.maximum(i * halo_blocks - 1, 0), 0)),
            _resident((1, d)),
            _resident(w_qkv.shape),
            _resident(w_z.shape),
            _resident(w_small.shape),
            _resident(conv_w.shape),
            _resident((1, LANES)),
            _resident((1, LANES)),
        ],
        out_specs=[row(kd), row(kd), row(kd), row(kd), row(LANES)],
        out_shape=[jax.ShapeDtypeStruct((t, kd), F32)] * 4 + [jax.ShapeDtypeStruct((t, LANES), F32)],
        scratch_shapes=[pltpu.VMEM((tm + CONV_HALO, kd), F32)],
        compiler_params=_params(("parallel",)),
        name="deltanet_in_proj",
    )(h, h, g, w_qkv, w_z, w_small, conv_w, a_log_pad, dt_bias_pad)


def _delta_kernel(q_ref, k_ref, v_ref, z_ref, bg_ref, gt_ref, ng_ref, o_ref, state_ref):
    c = q_ref.shape[1]
    heads = q_ref.shape[2] // DN_HEAD_DIM

    @pl.when(pl.program_id(1) == 0)
    def _():
        state_ref[...] = jnp.zeros_like(state_ref)

    row = lax.broadcasted_iota(jnp.int32, (c, c), 0)
    col = lax.broadcasted_iota(jnp.int32, (c, c), 1)
    causal = row >= col
    strict = row > col
    bg = bg_ref[0]
    g_cum_col = _dot_f32(causal.astype(F32), bg)
    g_cum_row = _dot_f32(gt_ref[0, 0], (row <= col).astype(F32))
    ng = ng_ref[...]
    for h in range(heads):
        hs = slice(h * DN_HEAD_DIM, (h + 1) * DN_HEAD_DIM)
        q = q_ref[0, :, hs]
        k = k_ref[0, :, hs]
        v = v_ref[0, :, hs]
        beta = bg[:, h:h + 1]
        gc = g_cum_col[:, heads + h:heads + h + 1]
        gr = g_cum_row[h:h + 1, :]
        decay = jnp.where(causal, jnp.exp(jnp.where(causal, gc - gr, 0.0)), 0.0)
        k_beta = k * beta
        kb16 = k.astype(BF16)
        scores = _dot_nt(jnp.concatenate([k_beta, q], axis=0).astype(BF16), kb16)
        qk = scores[c:] * decay
        x = jnp.where(strict, -(scores[:c] * decay), 0.0)
        inv = jnp.where(row == col, 1.0, x)
        p = 1
        while 2 * p < c:
            x16 = x.astype(BF16)
            x = _dot(x16, x16)
            inv = inv + _dot(inv.astype(BF16), x.astype(BF16))
            p *= 2
        e_gc = jnp.exp(gc)
        uw = _dot(inv.astype(BF16), jnp.concatenate([v * beta, k_beta * e_gc], axis=1).astype(BF16))
        state = state_ref[h]
        ws_qs = _dot(jnp.concatenate([uw[:, DN_HEAD_DIM:], q * e_gc], axis=0).astype(BF16),
                     state.astype(BF16))
        v_new = uw[:, :DN_HEAD_DIM] - ws_qs[:c]
        v_new16 = v_new.astype(BF16)
        o = ws_qs[c:] + _dot(qk.astype(BF16), v_new16)
        g_last = gc[c - 1:c, :]
        k_dec = k * jnp.exp(g_last - gc)
        state_ref[h] = state * jnp.exp(g_last) + _dot_tn(k_dec.astype(BF16), v_new16)
        o = o * lax.rsqrt(jnp.mean(o * o, axis=-1, keepdims=True) + RMS_EPS) * ng
        o_ref[0, :, hs] = (o * _silu(z_ref[0, :, hs])).astype(BF16)


def _delta_rule(q, k, v, z, bg, g_rows, norm_g):
    b, s, kd = q.shape
    heads = kd // DN_HEAD_DIM
    c = DN_CHUNK
    assert s % c == 0 and c & (c - 1) == 0
    blk = lambda n: pl.BlockSpec((1, c, n), lambda i, j: (i, j, 0))
    return pl.pallas_call(
        _delta_kernel,
        grid=(b, s // c),
        in_specs=[blk(kd), blk(kd), blk(kd), blk(kd), blk(LANES),
                  pl.BlockSpec((1, 1, heads, c), lambda i, j: (i, j, 0, 0)),
                  pl.BlockSpec((1, DN_HEAD_DIM), lambda i, j: (0, 0))],
        out_specs=blk(kd),
        out_shape=jax.ShapeDtypeStruct((b, s, kd), BF16),
        scratch_shapes=[pltpu.VMEM((heads, DN_HEAD_DIM, DN_HEAD_DIM), F32)],
        compiler_params=_params(("parallel", "arbitrary")),
        name="delta_rule",
    )(q, k, v, z, bg, g_rows, norm_g)


def _out_router_kernel(h_ref, o_ref, wo_ref, g_ref, rw_ref, rb_ref, h_out_ref, gates_ref):
    h = h_ref[...] + _dot(o_ref[...], wo_ref[...])
    h_out_ref[...] = h
    logits = _dot_f32(_rms(h, g_ref[...]), rw_ref[...]) + rb_ref[...]
    lane = lax.broadcasted_iota(jnp.int32, logits.shape, 1)
    gates = jnp.zeros_like(logits)
    remaining = logits
    tops, picks = [], []
    for _ in range(TOP_K):
        top = jnp.max(remaining, axis=1, keepdims=True)
        pick = jnp.min(jnp.where(remaining == top, lane, LANES), axis=1, keepdims=True)
        tops.append(top)
        picks.append(pick)
        remaining = jnp.where(lane == pick, -jnp.inf, remaining)
    exps = [jnp.exp(top - tops[0]) for top in tops]
    denom = sum(exps)
    for pick, ex in zip(picks, exps):
        gates = gates + jnp.where(lane == pick, ex / denom, 0.0)
    gates_ref[...] = gates


def _out_proj_router(h, o, w_out, g, router_w_pad, router_b_pad):
    t, d = h.shape
    tm = _pick_tile(t, 512, SUBLANES)
    row = lambda n: pl.BlockSpec((tm, n), lambda i: (i, 0))
    return pl.pallas_call(
        _out_router_kernel,
        grid=(t // tm,),
        in_specs=[row(d), row(o.shape[1]), _resident(w_out.shape), _resident((1, d)),
                  _resident(router_w_pad.shape), _resident((1, LANES))],
        out_specs=[row(d), row(LANES)],
        out_shape=[jax.ShapeDtypeStruct((t, d), F32), jax.ShapeDtypeStruct((t, LANES), F32)],
        compiler_params=_params(("parallel",)),
        name="out_proj_router",
    )(h, o, w_out, g, router_w_pad, router_b_pad)


def _pad_lanes(a, offset=0, fill=0.0):
    n = a.shape[-1]
    pad = [(0, 0)] * (a.ndim - 1) + [(offset, LANES - offset - n)]
    return jnp.pad(a, pad, constant_values=fill)


def kernel(x, norm_mix_g, norm_ffn_g, pool_w_in, pool_w_group, pool_scale, dn_w_in, dn_conv_w, dn_a_log, dn_dt_bias, dn_norm_g, dn_w_out, ffn_w_gate, ffn_w_up, ffn_w_down, moe_router_w, moe_router_b, moe_w_gate, moe_w_up, moe_w_down, final_norm_g):
    b, s, d = x.shape
    t = b * s
    kd = dn_w_out.shape[1]
    heads = kd // DN_HEAD_DIM
    n_experts = moe_router_w.shape[-1]
    assert 2 * heads <= LANES and n_experts <= LANES
    vec = lambda a: a.reshape(1, -1).astype(F32)

    h = x.reshape(t, d)

    h = _pool_mixer(h, s, vec(norm_mix_g[0]), pool_w_in[0].astype(BF16), pool_w_group[0].astype(BF16),
                    vec(pool_scale[0]))
    h = _ffn(h, vec(norm_ffn_g[0]), ffn_w_gate.astype(BF16), ffn_w_up.astype(BF16), ffn_w_down.astype(BF16),
             tm_target=1024, tf_target=1536)

    w_in = dn_w_in[0]
    w_small = _pad_lanes(w_in[:, 4 * kd:]).astype(BF16)
    q, k, v, z, bg = _dn_in_proj(
        h, s, vec(norm_mix_g[1]), w_in[:, :3 * kd].astype(BF16), w_in[:, 3 * kd:4 * kd].astype(BF16), w_small,
        dn_conv_w[0].astype(F32), _pad_lanes(vec(dn_a_log[0]), heads), _pad_lanes(vec(dn_dt_bias[0]), heads))
    g_rows = bg[:, heads:2 * heads].reshape(b, s // DN_CHUNK, DN_CHUNK, heads).transpose(0, 1, 3, 2)
    shape3 = lambda a: a.reshape(b, s, a.shape[-1])
    o = _delta_rule(shape3(q), shape3(k), shape3(v), shape3(z), shape3(bg), g_rows, vec(dn_norm_g[0]))
    h, gates = _out_proj_router(
        h, o.reshape(t, kd), dn_w_out[0].astype(BF16), vec(norm_ffn_g[1]),
        _pad_lanes(moe_router_w[0].astype(F32)), _pad_lanes(vec(moe_router_b[0]), fill=-jnp.inf))

    out = _ffn(h, vec(norm_ffn_g[1]), moe_w_gate[0].astype(BF16), moe_w_up[0].astype(BF16),
               moe_w_down[0].astype(BF16), gates=gates, final_g=vec(final_norm_g),
               tm_target=1024, tf_target=512)
    return out.reshape(b, s, d)
```

```python
import functools

import jax
import jax.numpy as jnp
from jax import lax
from jax.experimental import pallas as pl
from jax.experimental.pallas import tpu as pltpu

F32 = jnp.float32
BF16 = jnp.bfloat16

RMS_EPS = 1e-6
POOL_WINDOWS = (2, 4, 8, 16)
DN_HEAD_DIM = 128
DN_CHUNK = 64
TOP_K = 2

LANES = 128
SUBLANES = 8
VMEM_LIMIT_BYTES = 56 * 1024 * 1024

POOL_HALO = 16
CONV_HALO = SUBLANES

MOE_TILE_ROWS = 1024


def _dot(a, b):
    return jnp.dot(a, b, preferred_element_type=F32)


def _dot_f32(a, b):
    return jnp.dot(a, b, preferred_element_type=F32, precision=lax.Precision.HIGHEST)


def _dot_nt(a, b):
    return lax.dot_general(a, b, (((1,), (1,)), ((), ())), preferred_element_type=F32)


def _dot_tn(a, b):
    return lax.dot_general(a, b, (((0,), (0,)), ((), ())), preferred_element_type=F32)


def _rms(x, g):
    return x * lax.rsqrt(jnp.mean(x * x, axis=-1, keepdims=True) + RMS_EPS) * g


def _silu(x):
    return x * jax.nn.sigmoid(x)


def _pick_tile(n, target, quantum):
    if n <= target:
        return n
    best = None
    for t in range(quantum, target + 1, quantum):
        if n % t == 0:
            best = t
    assert best is not None, (n, target, quantum)
    return best


def _params(semantics):
    return pltpu.CompilerParams(dimension_semantics=semantics, vmem_limit_bytes=VMEM_LIMIT_BYTES)


def _resident(shape):
    nd = len(shape)
    return pl.BlockSpec(shape, lambda *_: (0,) * nd, pipeline_mode=pl.Buffered(1))


def _pool_kernel(x_ref, xh_ref, g_ref, win_ref, wgrp_ref, sc_ref, o_ref, ext_ref, *, tiles_per_seq):
    tm = x_ref.shape[0]
    j = pl.program_id(0) % tiles_per_seq
    x = x_ref[...]
    g = g_ref[...]
    win = win_ref[...]
    u = _dot(_rms(x, g).astype(BF16), win)
    uh = _dot(_rms(xh_ref[...], g).astype(BF16), win)
    ext_ref[0:POOL_HALO, :] = jnp.where(j == 0, 0.0, uh)
    ext_ref[POOL_HALO:, :] = u
    pos = (j * tm + lax.broadcasted_iota(jnp.int32, (tm, 1), 0) + 1).astype(F32)
    c = u.shape[1] // len(POOL_WINDOWS)
    for gi, w in enumerate(POOL_WINDOWS):
        sl = slice(gi * c, (gi + 1) * c)
        s = ext_ref[:, sl]
        sh = 1
        while sh < w:
            s = s + pltpu.roll(s, sh, 0)
            sh *= 2
        m = s[POOL_HALO:] / jnp.minimum(pos, float(w)) - u[:, sl]
        y = _dot(m.astype(BF16), wgrp_ref[gi])
        o_ref[:, sl] = x[:, sl] + y * sc_ref[:, sl]


def _pool_mixer(x2d, seq, g, w_in, w_group, scale):
    t, d = x2d.shape
    assert all(w & (w - 1) == 0 and w <= POOL_HALO for w in POOL_WINDOWS)
    tm = _pick_tile(seq, 512, POOL_HALO)
    assert tm % POOL_HALO == 0
    halo_blocks = tm // POOL_HALO
    return pl.pallas_call(
        functools.partial(_pool_kernel, tiles_per_seq=seq // tm),
        grid=(t // tm,),
        in_specs=[
            pl.BlockSpec((tm, d), lambda i: (i, 0)),
            pl.BlockSpec((POOL_HALO, d), lambda i: (jnp.maximum(i * halo_blocks - 1, 0), 0)),
            _resident((1, d)),
            _resident(w_in.shape),
            _resident(w_group.shape),
            _resident((1, d)),
        ],
        out_specs=pl.BlockSpec((tm, d), lambda i: (i, 0)),
        out_shape=jax.ShapeDtypeStruct((t, d), F32),
        scratch_shapes=[pltpu.VMEM((tm + POOL_HALO, d), F32)],
        compiler_params=_params(("parallel",)),
        name="pool_mixer",
    )(x2d, x2d, g, w_in, w_group, scale)


def _ffn_kernel(h_ref, g_ref, wg_ref, wu_ref, wd_ref, o_ref, hn_ref, acc_ref):
    f = pl.program_id(1)

    @pl.when(f == 0)
    def _():
        hn_ref[...] = _rms(h_ref[...], g_ref[...]).astype(BF16)
        acc_ref[...] = jnp.zeros_like(acc_ref)

    hn = hn_ref[...]
    act = _silu(_dot(hn, wg_ref[...])) * _dot(hn, wu_ref[...])
    acc_ref[...] += _dot(act.astype(BF16), wd_ref[...])

    @pl.when(f == pl.num_programs(1) - 1)
    def _():
        o_ref[...] = h_ref[...] + acc_ref[...]


def _ffn(h, g, w_gate, w_up, w_down):
    t, d = h.shape
    ff = w_gate.shape[1]
    tm = _pick_tile(t, 1024, SUBLANES)
    tf = _pick_tile(ff, 1536, LANES)
    row = pl.BlockSpec((tm, d), lambda i, f: (i, 0))
    return pl.pallas_call(
        _ffn_kernel,
        grid=(t // tm, ff // tf),
        in_specs=[row, pl.BlockSpec((1, d), lambda i, f: (0, 0)),
                  pl.BlockSpec((d, tf), lambda i, f: (0, f)),
                  pl.BlockSpec((d, tf), lambda i, f: (0, f)),
                  pl.BlockSpec((tf, d), lambda i, f: (f, 0))],
        out_specs=row,
        out_shape=jax.ShapeDtypeStruct((t, d), F32),
        scratch_shapes=[pltpu.VMEM((tm, d), BF16), pltpu.VMEM((tm, d), F32)],
        compiler_params=_params(("parallel", "arbitrary")),
        name="dense_ffn",
    )(h, g, w_gate, w_up, w_down)


def _dn_in_kernel(h_ref, hh_ref, g_ref, wqkv_ref, wz_ref, wsm_ref, cw_ref, alog_ref, dtb_ref,
                  q_ref, k_ref, v_ref, z_ref, bg_ref, ext_ref, *, tiles_per_seq):
    j = pl.program_id(0) % tiles_per_seq
    g = g_ref[...]
    hn = _rms(h_ref[...], g).astype(BF16)
    hnh = jnp.where(j == 0, 0.0, _rms(hh_ref[...], g)).astype(BF16)
    kd = q_ref.shape[1]
    heads = kd // DN_HEAD_DIM
    taps = cw_ref.shape[0]
    for part, out_ref in enumerate((q_ref, k_ref, v_ref)):
        cols = slice(part * kd, (part + 1) * kd)
        w = wqkv_ref[:, cols]
        ext_ref[0:CONV_HALO, :] = _dot(hnh, w)
        ext_ref[CONV_HALO:, :] = _dot(hn, w)
        p = ext_ref[...]
        cw = cw_ref[:, cols]
        y = p * cw[taps - 1:taps, :]
        for s in range(1, taps):
            y = y + pltpu.roll(p, s, 0) * cw[taps - 1 - s:taps - s, :]
        y = _silu(y[CONV_HALO:])
        if part == 2:
            out_ref[...] = y
            continue
        for h in range(heads):
            hs = slice(h * DN_HEAD_DIM, (h + 1) * DN_HEAD_DIM)
            yh = y[:, hs]
            yh = yh * lax.rsqrt(jnp.sum(yh * yh, axis=-1, keepdims=True) + RMS_EPS)
            if part == 0:
                yh = yh * (DN_HEAD_DIM ** -0.5)
            out_ref[:, hs] = yh
    z_ref[...] = _dot(hn, wz_ref[...])
    sm = _dot(hn, wsm_ref[...])
    xs = sm + dtb_ref[...]
    softplus = jnp.maximum(xs, 0.0) + jnp.log1p(jnp.exp(-jnp.abs(xs)))
    lane = lax.broadcasted_iota(jnp.int32, sm.shape, 1)
    bg_ref[...] = jnp.where(lane < heads, jax.nn.sigmoid(sm), -jnp.exp(alog_ref[...]) * softplus)


def _dn_in_proj(h, seq, g, w_qkv, w_z, w_small, conv_w, a_log_pad, dt_bias_pad):
    t, d = h.shape
    kd = w_z.shape[1]
    assert conv_w.shape[0] - 1 <= CONV_HALO
    tm = _pick_tile(seq, 512, CONV_HALO)
    halo_blocks = tm // CONV_HALO
    row = lambda n: pl.BlockSpec((tm, n), lambda i: (i, 0))
    return pl.pallas_call(
        functools.partial(_dn_in_kernel, tiles_per_seq=seq // tm),
        grid=(t // tm,),
        in_specs=[
            row(d),
            pl.BlockSpec((CONV_HALO, d), lambda i: (jnp.maximum(i * halo_blocks - 1, 0), 0)),
            _resident((1, d)),
            _resident(w_qkv.shape),
            _resident(w_z.shape),
            _resident(w_small.shape),
            _resident(conv_w.shape),
            _resident((1, LANES)),
            _resident((1, LANES)),
        ],
        out_specs=[row(kd), row(kd), row(kd), row(kd), row(LANES)],
        out_shape=[jax.ShapeDtypeStruct((t, kd), F32)] * 4 + [jax.ShapeDtypeStruct((t, LANES), F32)],
        scratch_shapes=[pltpu.VMEM((tm + CONV_HALO, kd), F32)],
        compiler_params=_params(("parallel",)),
        name="deltanet_in_proj",
    )(h, h, g, w_qkv, w_z, w_small, conv_w, a_log_pad, dt_bias_pad)


def _delta_kernel(q_ref, k_ref, v_ref, z_ref, bg_ref, gt_ref, ng_ref, o_ref, state_ref):
    nb, c, kd = q_ref.shape
    heads = kd // DN_HEAD_DIM
    pairs = [(b, h) for b in range(nb) for h in range(heads)]

    @pl.when(pl.program_id(1) == 0)
    def _():
        state_ref[...] = jnp.zeros_like(state_ref)

    def head_cols(h):
        return slice(h * DN_HEAD_DIM, (h + 1) * DN_HEAD_DIM)

    row = lax.broadcasted_iota(jnp.int32, (c, c), 0)
    col = lax.broadcasted_iota(jnp.int32, (c, c), 1)
    causal = row >= col
    strict = row > col
    diag = row == col
    lower_ones = causal.astype(F32)
    upper_ones = (row <= col).astype(F32)
    bg = [bg_ref[b] for b in range(nb)]
    g_cum_col = [_dot_f32(lower_ones, bg[b]) for b in range(nb)]
    g_cum_row = [_dot_f32(gt_ref[b, 0], upper_ones) for b in range(nb)]

    q = [q_ref[b, :, head_cols(h)] for b, h in pairs]
    k = [k_ref[b, :, head_cols(h)] for b, h in pairs]
    beta = [bg[b][:, h:h + 1] for b, h in pairs]
    gc = [g_cum_col[b][:, heads + h:heads + h + 1] for b, h in pairs]
    gr = [g_cum_row[b][h:h + 1, :] for b, h in pairs]
    n = len(pairs)
    rng = range(n)
    decay = [jnp.where(causal, jnp.exp(jnp.where(causal, gc[i] - gr[i], 0.0)), 0.0) for i in rng]
    k_beta = [k[i] * beta[i] for i in rng]
    scores = [_dot_nt(jnp.concatenate([k_beta[i], q[i]], axis=0).astype(BF16), k[i].astype(BF16))
              for i in rng]
    qk = [(scores[i][c:] * decay[i]).astype(BF16) for i in rng]
    x = [jnp.where(strict, -(scores[i][:c] * decay[i]), 0.0) for i in rng]
    inv = [jnp.where(diag, 1.0, x[i]) for i in rng]
    p = 1
    while 2 * p < c:
        x16 = [x[i].astype(BF16) for i in rng]
        x = [_dot(x16[i], x16[i]) for i in rng]
        inv = [inv[i] + _dot(inv[i].astype(BF16), x[i].astype(BF16)) for i in rng]
        p *= 2
    e_gc = [jnp.exp(gc[i]) for i in rng]
    rhs = [jnp.concatenate([v_ref[b, :, head_cols(h)] * beta[i], k_beta[i] * e_gc[i]], axis=1).astype(BF16)
           for i, (b, h) in enumerate(pairs)]
    uw = [_dot(inv[i].astype(BF16), rhs[i]) for i in rng]
    state = [state_ref[i] for i in rng]
    ws_qs = [_dot(jnp.concatenate([uw[i][:, DN_HEAD_DIM:], q[i] * e_gc[i]], axis=0).astype(BF16),
                  state[i].astype(BF16)) for i in rng]
    v_new = [(uw[i][:, :DN_HEAD_DIM] - ws_qs[i][:c]).astype(BF16) for i in rng]
    o = [ws_qs[i][c:] + _dot(qk[i], v_new[i]) for i in rng]
    g_last = [gc[i][c - 1:c, :] for i in rng]
    k_dec = [(k[i] * jnp.exp(g_last[i] - gc[i])).astype(BF16) for i in rng]
    for i in rng:
        state_ref[i] = state[i] * jnp.exp(g_last[i]) + _dot_tn(k_dec[i], v_new[i])
    ng = ng_ref[...]
    for i, (b, h) in enumerate(pairs):
        on = o[i] * lax.rsqrt(jnp.mean(o[i] * o[i], axis=-1, keepdims=True) + RMS_EPS) * ng
        o_ref[b, :, head_cols(h)] = (on * _silu(z_ref[b, :, head_cols(h)])).astype(BF16)


def _delta_rule(q, k, v, z, bg, g_rows, norm_g, *, seqs_per_step):
    b, s, kd = q.shape
    heads = kd // DN_HEAD_DIM
    c = DN_CHUNK
    nb = seqs_per_step
    assert s % c == 0 and c & (c - 1) == 0 and b % nb == 0
    blk = lambda n: pl.BlockSpec((nb, c, n), lambda i, j: (i, j, 0))
    return pl.pallas_call(
        _delta_kernel,
        grid=(b // nb, s // c),
        in_specs=[blk(kd), blk(kd), blk(kd), blk(kd), blk(LANES),
                  pl.BlockSpec((nb, 1, heads, c), lambda i, j: (i, j, 0, 0)),
                  pl.BlockSpec((1, DN_HEAD_DIM), lambda i, j: (0, 0))],
        out_specs=blk(kd),
        out_shape=jax.ShapeDtypeStruct((b, s, kd), BF16),
        scratch_shapes=[pltpu.VMEM((nb * heads, DN_HEAD_DIM, DN_HEAD_DIM), F32)],
        compiler_params=_params(("parallel", "arbitrary")),
        name="delta_rule",
    )(q, k, v, z, bg, g_rows, norm_g)


def _out_router_kernel(h_ref, o_ref, wo_ref, g_ref, rw_ref, rb_ref, h_out_ref, route_ref):
    h = h_ref[...] + _dot(o_ref[...], wo_ref[...])
    h_out_ref[...] = h
    logits = _dot_f32(_rms(h, g_ref[...]), rw_ref[...]) + rb_ref[...]
    lane = lax.broadcasted_iota(jnp.int32, logits.shape, 1)
    remaining = logits
    tops, picks = [], []
    for _ in range(TOP_K):
        top = jnp.max(remaining, axis=1, keepdims=True)
        pick = jnp.min(jnp.where(remaining == top, lane, LANES), axis=1, keepdims=True)
        tops.append(top)
        picks.append(pick)
        remaining = jnp.where(lane == pick, -jnp.inf, remaining)
    exps = [jnp.exp(top - tops[0]) for top in tops]
    denom = sum(exps)
    route = jnp.zeros_like(logits)
    for j, (pick, ex) in enumerate(zip(picks, exps)):
        route = jnp.where(lane == j, pick.astype(F32), route)
        route = jnp.where(lane == TOP_K + j, ex / denom, route)
    route_ref[...] = route


def _out_proj_router(h, o, w_out, g, router_w_pad, router_b_pad):
    t, d = h.shape
    tm = _pick_tile(t, 512, SUBLANES)
    row = lambda n: pl.BlockSpec((tm, n), lambda i: (i, 0))
    return pl.pallas_call(
        _out_router_kernel,
        grid=(t // tm,),
        in_specs=[row(d), row(o.shape[1]), _resident(w_out.shape), _resident((1, d)),
                  _resident(router_w_pad.shape), _resident((1, LANES))],
        out_specs=[row(d), row(LANES)],
        out_shape=[jax.ShapeDtypeStruct((t, d), F32), jax.ShapeDtypeStruct((t, LANES), F32)],
        compiler_params=_params(("parallel",)),
        name="out_proj_router",
    )(h, o, w_out, g, router_w_pad, router_b_pad)


def _routing_tables(route, n_experts, tile_rows):
    t = route.shape[0]
    assert (TOP_K * t) % tile_rows == 0
    idx = route[:, :TOP_K].astype(jnp.int32)
    member = (idx[:, :, None] == jnp.arange(n_experts, dtype=jnp.int32)).any(axis=1).astype(jnp.int32)
    counts = member.sum(axis=0)
    rank = jnp.cumsum(member, axis=0) - member
    padded = (counts + tile_rows - 1) // tile_rows * tile_rows
    ends = jnp.cumsum(padded)
    starts = ends - padded
    slot = jnp.take_along_axis(starts[None, :] + rank, idx, axis=1).astype(jnp.int32)
    n_tiles = TOP_K * t // tile_rows + n_experts
    n_used = (ends[-1] // tile_rows).astype(jnp.int32)
    tile_start = jnp.arange(n_tiles, dtype=jnp.int32) * tile_rows
    tile_expert = jnp.sum(tile_start[:, None] >= ends[None, :], axis=1).astype(jnp.int32)
    tile_expert = jnp.where(jnp.arange(n_tiles) < n_used, tile_expert, tile_expert[n_used - 1])
    return slot, tile_expert, n_used.reshape(1)


def _row_copy(src_ref, src_row, dst_ref, dst_row, sem):
    return pltpu.make_async_copy(src_ref.at[pl.ds(src_row, 1)], dst_ref.at[pl.ds(dst_row, 1)], sem)


def _dispatch_kernel(slot_ref, h_hbm, xs_init_hbm, xs_hbm, sem):
    del xs_init_hbm
    tm = slot_ref.shape[0] // TOP_K
    base = pl.program_id(0) * tm

    def copies(i):
        return [_row_copy(h_hbm, base + i, xs_hbm, slot_ref[TOP_K * i + j], sem) for j in range(TOP_K)]

    @pl.loop(0, tm)
    def _(i):
        for cp in copies(i):
            cp.start()

    @pl.loop(0, tm)
    def _(i):
        for cp in copies(i):
            cp.wait()


def _dispatch(h, slot_flat, n_rows):
    t, d = h.shape
    tm = _pick_tile(t, 512, SUBLANES)
    return pl.pallas_call(
        _dispatch_kernel,
        grid=(t // tm,),
        in_specs=[pl.BlockSpec((TOP_K * tm,), lambda i: (i,), memory_space=pltpu.SMEM),
                  pl.BlockSpec(memory_space=pl.ANY), pl.BlockSpec(memory_space=pl.ANY)],
        out_specs=pl.BlockSpec(memory_space=pl.ANY),
        out_shape=jax.ShapeDtypeStruct((n_rows, d), F32),
        scratch_shapes=[pltpu.SemaphoreType.DMA(())],
        input_output_aliases={2: 0},
        compiler_params=_params(("arbitrary",)),
        name="moe_dispatch",
    )(slot_flat, h, jnp.zeros((n_rows, d), F32))


def _grouped_ffn_kernel(te_ref, nu_ref, x_ref, g_ref, wg_ref, wu_ref, wd_ref, y_ref, xn_ref):
    del te_ref
    f = pl.program_id(1)
    used = pl.program_id(0) < nu_ref[0]

    @pl.when(jnp.logical_not(used) & (f == 0))
    def _():
        y_ref[...] = jnp.zeros_like(y_ref)

    @pl.when(used)
    def _():
        @pl.when(f == 0)
        def _():
            xn_ref[...] = _rms(x_ref[...], g_ref[...]).astype(BF16)

        xn = xn_ref[...]
        act = _silu(_dot(xn, wg_ref[0])) * _dot(xn, wu_ref[0])
        y = _dot(act.astype(BF16), wd_ref[0])

        @pl.when(f == 0)
        def _():
            y_ref[...] = y

        @pl.when(f > 0)
        def _():
            y_ref[...] += y


def _grouped_ffn(xs, g, w_gate, w_up, w_down, tile_expert, n_used, *, tile_rows, tf_target):
    n_rows, d = xs.shape
    ff = w_gate.shape[2]
    tf = _pick_tile(ff, tf_target, LANES)
    nf = ff // tf
    n_tiles = n_rows // tile_rows

    def row_map(i, f, te, nu):
        return (jnp.minimum(i, nu[0] - 1), 0)

    def ff_index(i, f, nu):
        return jnp.where(i < nu[0], f, nf - 1)

    return pl.pallas_call(
        _grouped_ffn_kernel,
        grid_spec=pltpu.PrefetchScalarGridSpec(
            num_scalar_prefetch=2,
            grid=(n_tiles, nf),
            in_specs=[
                pl.BlockSpec((tile_rows, d), row_map),
                pl.BlockSpec((1, d), lambda i, f, te, nu: (0, 0)),
                pl.BlockSpec((1, d, tf), lambda i, f, te, nu: (te[i], 0, ff_index(i, f, nu))),
                pl.BlockSpec((1, d, tf), lambda i, f, te, nu: (te[i], 0, ff_index(i, f, nu))),
                pl.BlockSpec((1, tf, d), lambda i, f, te, nu: (te[i], ff_index(i, f, nu), 0)),
            ],
            out_specs=pl.BlockSpec((tile_rows, d), lambda i, f, te, nu: (i, 0)),
            scratch_shapes=[pltpu.VMEM((tile_rows, d), BF16)],
        ),
        out_shape=jax.ShapeDtypeStruct((n_rows, d), F32),
        compiler_params=_params(("arbitrary", "arbitrary")),
        name="moe_grouped_ffn",
    )(tile_expert, n_used, xs, g, w_gate, w_up, w_down)


def _combine_kernel(slot_ref, h_ref, route_ref, gf_ref, y_hbm, o_ref, ybuf_ref, sem):
    tm = h_ref.shape[0]

    def copies(i):
        return [_row_copy(y_hbm, slot_ref[TOP_K * i + j], ybuf_ref.at[j], i, sem) for j in range(TOP_K)]

    @pl.loop(0, tm)
    def _(i):
        for cp in copies(i):
            cp.start()

    @pl.loop(0, tm)
    def _(i):
        for cp in copies(i):
            cp.wait()

    route = route_ref[...]
    out = h_ref[...]
    for j in range(TOP_K):
        out = out + route[:, TOP_K + j:TOP_K + j + 1] * ybuf_ref[j]
    o_ref[...] = _rms(out, gf_ref[...])


def _combine(h, route, slot_flat, y, final_g):
    t, d = h.shape
    tm = _pick_tile(t, 256, SUBLANES)
    row = lambda n: pl.BlockSpec((tm, n), lambda i: (i, 0))
    return pl.pallas_call(
        _combine_kernel,
        grid=(t // tm,),
        in_specs=[pl.BlockSpec((TOP_K * tm,), lambda i: (i,), memory_space=pltpu.SMEM),
                  row(d), row(LANES), pl.BlockSpec((1, d), lambda i: (0, 0)),
                  pl.BlockSpec(memory_space=pl.ANY)],
        out_specs=row(d),
        out_shape=jax.ShapeDtypeStruct((t, d), F32),
        scratch_shapes=[pltpu.VMEM((TOP_K, tm, d), F32), pltpu.SemaphoreType.DMA(())],
        compiler_params=_params(("arbitrary",)),
        name="moe_combine",
    )(slot_flat, h, route, final_g, y)


def _pad_lanes(a, offset=0, fill=0.0):
    n = a.shape[-1]
    pad = [(0, 0)] * (a.ndim - 1) + [(offset, LANES - offset - n)]
    return jnp.pad(a, pad, constant_values=fill)


def kernel(x, norm_mix_g, norm_ffn_g, pool_w_in, pool_w_group, pool_scale, dn_w_in, dn_conv_w, dn_a_log, dn_dt_bias, dn_norm_g, dn_w_out, ffn_w_gate, ffn_w_up, ffn_w_down, moe_router_w, moe_router_b, moe_w_gate, moe_w_up, moe_w_down, final_norm_g):
    b, s, d = x.shape
    t = b * s
    kd = dn_w_out.shape[1]
    heads = kd // DN_HEAD_DIM
    n_experts = moe_router_w.shape[-1]
    assert 2 * heads <= LANES and n_experts <= LANES and 2 * TOP_K <= LANES
    vec = lambda a: a.reshape(1, -1).astype(F32)

    h = x.reshape(t, d)

    h = _pool_mixer(h, s, vec(norm_mix_g[0]), pool_w_in[0].astype(BF16), pool_w_group[0].astype(BF16),
                    vec(pool_scale[0]))
    h = _ffn(h, vec(norm_ffn_g[0]), ffn_w_gate[0].astype(BF16), ffn_w_up[0].astype(BF16),
             ffn_w_down[0].astype(BF16))

    w_in = dn_w_in[0]
    w_small = _pad_lanes(w_in[:, 4 * kd:]).astype(BF16)
    q, k, v, z, bg = _dn_in_proj(
        h, s, vec(norm_mix_g[1]), w_in[:, :3 * kd].astype(BF16), w_in[:, 3 * kd:4 * kd].astype(BF16), w_small,
        dn_conv_w[0].astype(F32), _pad_lanes(vec(dn_a_log[0]), heads), _pad_lanes(vec(dn_dt_bias[0]), heads))
    g_rows = bg[:, heads:2 * heads].reshape(b, s // DN_CHUNK, DN_CHUNK, heads).transpose(0, 1, 3, 2)
    shape3 = lambda a: a.reshape(b, s, a.shape[-1])
    o = _delta_rule(shape3(q), shape3(k), shape3(v), shape3(z), shape3(bg), g_rows, vec(dn_norm_g[0]),
                    seqs_per_step=2 if b % 2 == 0 else 1)
    h, route = _out_proj_router(
        h, o.reshape(t, kd), dn_w_out[0].astype(BF16), vec(norm_ffn_g[1]),
        _pad_lanes(moe_router_w[0].astype(F32)), _pad_lanes(vec(moe_router_b[0]), fill=-jnp.inf))

    tile_rows = _pick_tile(TOP_K * t, MOE_TILE_ROWS, SUBLANES)
    slot, tile_expert, n_used = _routing_tables(route, n_experts, tile_rows)
    slot_flat = slot.reshape(-1)
    xs = _dispatch(h, slot_flat, tile_expert.shape[0] * tile_rows)
    y = _grouped_ffn(xs, vec(norm_ffn_g[1]), moe_w_gate[0].astype(BF16), moe_w_up[0].astype(BF16),
                     moe_w_down[0].astype(BF16), tile_expert, n_used, tile_rows=tile_rows, tf_target=512)
    out = _combine(h, route, slot_flat, y, vec(final_norm_g))
    return out.reshape(b, s, d)
```

```python
import functools

import jax
import jax.numpy as jnp
from jax import lax
from jax.experimental import pallas as pl
from jax.experimental.pallas import tpu as pltpu

F32 = jnp.float32
BF16 = jnp.bfloat16

RMS_EPS = 1e-6
POOL_WINDOWS = (2, 4, 8, 16)
DN_HEAD_DIM = 128
DN_CHUNK = 64
TOP_K = 2

LANES = 128
SUBLANES = 8
VMEM_LIMIT_BYTES = 56 * 1024 * 1024

POOL_HALO = 16
CONV_HALO = SUBLANES

MOE_TILE_ROWS = 1024


def _dot(a, b):
    return jnp.dot(a, b, preferred_element_type=F32)


def _dot_f32(a, b):
    return jnp.dot(a, b, preferred_element_type=F32, precision=lax.Precision.HIGHEST)


def _dot_nt(a, b):
    return lax.dot_general(a, b, (((1,), (1,)), ((), ())), preferred_element_type=F32)


def _dot_tn(a, b):
    return lax.dot_general(a, b, (((0,), (0,)), ((), ())), preferred_element_type=F32)


def _rms(x, g):
    return x * lax.rsqrt(jnp.mean(x * x, axis=-1, keepdims=True) + RMS_EPS) * g


def _silu(x):
    return x * jax.nn.sigmoid(x)


def _pick_tile(n, target, quantum):
    if n <= target:
        return n
    best = None
    for t in range(quantum, target + 1, quantum):
        if n % t == 0:
            best = t
    assert best is not None, (n, target, quantum)
    return best


def _params(semantics):
    return pltpu.CompilerParams(dimension_semantics=semantics, vmem_limit_bytes=VMEM_LIMIT_BYTES)


def _resident(shape):
    nd = len(shape)
    return pl.BlockSpec(shape, lambda *_: (0,) * nd, pipeline_mode=pl.Buffered(1))


def _pool_kernel(x_ref, xh_ref, g_ref, win_ref, wgrp_ref, sc_ref, o_ref, ext_ref, *, tiles_per_seq):
    tm = x_ref.shape[0]
    j = pl.program_id(0) % tiles_per_seq
    x = x_ref[...]
    g = g_ref[...]
    win = win_ref[...]
    u = _dot(_rms(x, g).astype(BF16), win)
    uh = _dot(_rms(xh_ref[...], g).astype(BF16), win)
    ext_ref[0:POOL_HALO, :] = jnp.where(j == 0, 0.0, uh)
    ext_ref[POOL_HALO:, :] = u
    pos = (j * tm + lax.broadcasted_iota(jnp.int32, (tm, 1), 0) + 1).astype(F32)
    c = u.shape[1] // len(POOL_WINDOWS)
    for gi, w in enumerate(POOL_WINDOWS):
        sl = slice(gi * c, (gi + 1) * c)
        s = ext_ref[:, sl]
        sh = 1
        while sh < w:
            s = s + pltpu.roll(s, sh, 0)
            sh *= 2
        m = s[POOL_HALO:] / jnp.minimum(pos, float(w)) - u[:, sl]
        y = _dot(m.astype(BF16), wgrp_ref[gi])
        o_ref[:, sl] = x[:, sl] + y * sc_ref[:, sl]


def _pool_mixer(x2d, seq, g, w_in, w_group, scale):
    t, d = x2d.shape
    assert all(w & (w - 1) == 0 and w <= POOL_HALO for w in POOL_WINDOWS)
    tm = _pick_tile(seq, 512, POOL_HALO)
    assert tm % POOL_HALO == 0
    halo_blocks = tm // POOL_HALO
    return pl.pallas_call(
        functools.partial(_pool_kernel, tiles_per_seq=seq // tm),
        grid=(t // tm,),
        in_specs=[
            pl.BlockSpec((tm, d), lambda i: (i, 0)),
            pl.BlockSpec((POOL_HALO, d), lambda i: (jnp.maximum(i * halo_blocks - 1, 0), 0)),
            _resident((1, d)),
            _resident(w_in.shape),
            _resident(w_group.shape),
            _resident((1, d)),
        ],
        out_specs=pl.BlockSpec((tm, d), lambda i: (i, 0)),
        out_shape=jax.ShapeDtypeStruct((t, d), F32),
        scratch_shapes=[pltpu.VMEM((tm + POOL_HALO, d), F32)],
        compiler_params=_params(("parallel",)),
        name="pool_mixer",
    )(x2d, x2d, g, w_in, w_group, scale)


def _ffn_kernel(h_ref, g_ref, wg_ref, wu_ref, wd_ref, o_ref, hn_ref, acc_ref):
    f = pl.program_id(1)

    @pl.when(f == 0)
    def _():
        hn_ref[...] = _rms(h_ref[...], g_ref[...]).astype(BF16)
        acc_ref[...] = jnp.zeros_like(acc_ref)

    hn = hn_ref[...]
    act = _silu(_dot(hn, wg_ref[...])) * _dot(hn, wu_ref[...])
    acc_ref[...] += _dot(act.astype(BF16), wd_ref[...])

    @pl.when(f == pl.num_programs(1) - 1)
    def _():
        o_ref[...] = h_ref[...] + acc_ref[...]


def _ffn(h, g, w_gate, w_up, w_down):
    t, d = h.shape
    ff = w_gate.shape[1]
    tm = _pick_tile(t, 1024, SUBLANES)
    tf = _pick_tile(ff, 1536, LANES)
    row = pl.BlockSpec((tm, d), lambda i, f: (i, 0))
    return pl.pallas_call(
        _ffn_kernel,
        grid=(t // tm, ff // tf),
        in_specs=[row, pl.BlockSpec((1, d), lambda i, f: (0, 0)),
                  pl.BlockSpec((d, tf), lambda i, f: (0, f)),
                  pl.BlockSpec((d, tf), lambda i, f: (0, f)),
                  pl.BlockSpec((tf, d), lambda i, f: (f, 0))],
        out_specs=row,
        out_shape=jax.ShapeDtypeStruct((t, d), F32),
        scratch_shapes=[pltpu.VMEM((tm, d), BF16), pltpu.VMEM((tm, d), F32)],
        compiler_params=_params(("parallel", "arbitrary")),
        name="dense_ffn",
    )(h, g, w_gate, w_up, w_down)


def _dn_in_kernel(h_ref, hh_ref, g_ref, wqkv_ref, wz_ref, wsm_ref, cw_ref, alog_ref, dtb_ref,
                  q_ref, k_ref, v_ref, z_ref, bg_ref, ext_ref, *, tiles_per_seq):
    j = pl.program_id(0) % tiles_per_seq
    g = g_ref[...]
    hn = _rms(h_ref[...], g).astype(BF16)
    hnh = jnp.where(j == 0, 0.0, _rms(hh_ref[...], g)).astype(BF16)
    kd = q_ref.shape[1]
    heads = kd // DN_HEAD_DIM
    taps = cw_ref.shape[0]
    for part, out_ref in enumerate((q_ref, k_ref, v_ref)):
        cols = slice(part * kd, (part + 1) * kd)
        w = wqkv_ref[:, cols]
        ext_ref[0:CONV_HALO, :] = _dot(hnh, w)
        ext_ref[CONV_HALO:, :] = _dot(hn, w)
        p = ext_ref[...]
        cw = cw_ref[:, cols]
        y = p * cw[taps - 1:taps, :]
        for s in range(1, taps):
            y = y + pltpu.roll(p, s, 0) * cw[taps - 1 - s:taps - s, :]
        y = _silu(y[CONV_HALO:])
        if part == 2:
            out_ref[...] = y
            continue
        for h in range(heads):
            hs = slice(h * DN_HEAD_DIM, (h + 1) * DN_HEAD_DIM)
            yh = y[:, hs]
            yh = yh * lax.rsqrt(jnp.sum(yh * yh, axis=-1, keepdims=True) + RMS_EPS)
            if part == 0:
                yh = yh * (DN_HEAD_DIM ** -0.5)
            out_ref[:, hs] = yh
    z_ref[...] = _dot(hn, wz_ref[...])
    sm = _dot(hn, wsm_ref[...])
    xs = sm + dtb_ref[...]
    softplus = jnp.maximum(xs, 0.0) + jnp.log1p(jnp.exp(-jnp.abs(xs)))
    lane = lax.broadcasted_iota(jnp.int32, sm.shape, 1)
    bg_ref[...] = jnp.where(lane < heads, jax.nn.sigmoid(sm), -jnp.exp(alog_ref[...]) * softplus)


def _dn_in_proj(h, seq, g, w_qkv, w_z, w_small, conv_w, a_log_pad, dt_bias_pad):
    t, d = h.shape
    kd = w_z.shape[1]
    assert conv_w.shape[0] - 1 <= CONV_HALO
    tm = _pick_tile(seq, 512, CONV_HALO)
    halo_blocks = tm // CONV_HALO
    row = lambda n: pl.BlockSpec((tm, n), lambda i: (i, 0))
    return pl.pallas_call(
        functools.partial(_dn_in_kernel, tiles_per_seq=seq // tm),
        grid=(t // tm,),
        in_specs=[
            row(d),
            pl.BlockSpec((CONV_HALO, d), lambda i: (jnp.maximum(i * halo_blocks - 1, 0), 0)),
            _resident((1, d)),
            _resident(w_qkv.shape),
            _resident(w_z.shape),
            _resident(w_small.shape),
            _resident(conv_w.shape),
            _resident((1, LANES)),
            _resident((1, LANES)),
        ],
        out_specs=[row(kd), row(kd), row(kd), row(kd), row(LANES)],
        out_shape=[jax.ShapeDtypeStruct((t, kd), F32)] * 4 + [jax.ShapeDtypeStruct((t, LANES), F32)],
        scratch_shapes=[pltpu.VMEM((tm + CONV_HALO, kd), F32)],
        compiler_params=_params(("parallel",)),
        name="deltanet_in_proj",
    )(h, h, g, w_qkv, w_z, w_small, conv_w, a_log_pad, dt_bias_pad)


def _delta_kernel(q_ref, k_ref, v_ref, z_ref, bg_ref, gt_ref, ng_ref, o_ref, state_ref):
    nb, c, kd = q_ref.shape
    heads = kd // DN_HEAD_DIM
    pairs = [(b, h) for b in range(nb) for h in range(heads)]

    @pl.when(pl.program_id(1) == 0)
    def _():
        state_ref[...] = jnp.zeros_like(state_ref)

    def head_cols(h):
        return slice(h * DN_HEAD_DIM, (h + 1) * DN_HEAD_DIM)

    row = lax.broadcasted_iota(jnp.int32, (c, c), 0)
    col = lax.broadcasted_iota(jnp.int32, (c, c), 1)
    causal = row >= col
    strict = row > col
    diag = row == col
    lower_ones = causal.astype(F32)
    upper_ones = (row <= col).astype(F32)
    bg = [bg_ref[b] for b in range(nb)]
    g_cum_col = [_dot_f32(lower_ones, bg[b]) for b in range(nb)]
    g_cum_row = [_dot_f32(gt_ref[b, 0], upper_ones) for b in range(nb)]

    q = [q_ref[b, :, head_cols(h)] for b, h in pairs]
    k = [k_ref[b, :, head_cols(h)] for b, h in pairs]
    beta = [bg[b][:, h:h + 1] for b, h in pairs]
    gc = [g_cum_col[b][:, heads + h:heads + h + 1] for b, h in pairs]
    gr = [g_cum_row[b][h:h + 1, :] for b, h in pairs]
    n = len(pairs)
    rng = range(n)
    decay = [jnp.where(causal, jnp.exp(jnp.where(causal, gc[i] - gr[i], 0.0)), 0.0) for i in rng]
    k_beta = [k[i] * beta[i] for i in rng]
    scores = [_dot_nt(jnp.concatenate([k_beta[i], q[i]], axis=0).astype(BF16), k[i].astype(BF16))
              for i in rng]
    qk = [(scores[i][c:] * decay[i]).astype(BF16) for i in rng]
    x = [jnp.where(strict, -(scores[i][:c] * decay[i]), 0.0) for i in rng]
    inv = [jnp.where(diag, 1.0, x[i]) for i in rng]
    p = 1
    while 2 * p < c:
        x16 = [x[i].astype(BF16) for i in rng]
        x = [_dot(x16[i], x16[i]) for i in rng]
        inv = [inv[i] + _dot(inv[i].astype(BF16), x[i].astype(BF16)) for i in rng]
        p *= 2
    e_gc = [jnp.exp(gc[i]) for i in rng]
    rhs = [jnp.concatenate([v_ref[b, :, head_cols(h)] * beta[i], k_beta[i] * e_gc[i]], axis=1).astype(BF16)
           for i, (b, h) in enumerate(pairs)]
    uw = [_dot(inv[i].astype(BF16), rhs[i]) for i in rng]
    state = [state_ref[i] for i in rng]
    ws_qs = [_dot(jnp.concatenate([uw[i][:, DN_HEAD_DIM:], q[i] * e_gc[i]], axis=0).astype(BF16),
                  state[i].astype(BF16)) for i in rng]
    v_new = [(uw[i][:, :DN_HEAD_DIM] - ws_qs[i][:c]).astype(BF16) for i in rng]
    o = [ws_qs[i][c:] + _dot(qk[i], v_new[i]) for i in rng]
    g_last = [gc[i][c - 1:c, :] for i in rng]
    k_dec = [(k[i] * jnp.exp(g_last[i] - gc[i])).astype(BF16) for i in rng]
    for i in rng:
        state_ref[i] = state[i] * jnp.exp(g_last[i]) + _dot_tn(k_dec[i], v_new[i])
    ng = ng_ref[...]
    for i, (b, h) in enumerate(pairs):
        on = o[i] * lax.rsqrt(jnp.mean(o[i] * o[i], axis=-1, keepdims=True) + RMS_EPS) * ng
        o_ref[b, :, head_cols(h)] = (on * _silu(z_ref[b, :, head_cols(h)])).astype(BF16)


def _delta_rule(q, k, v, z, bg, g_rows, norm_g, *, seqs_per_step):
    b, s, kd = q.shape
    heads = kd // DN_HEAD_DIM
    c = DN_CHUNK
    nb = seqs_per_step
    assert s % c == 0 and c & (c - 1) == 0 and b % nb == 0
    blk = lambda n: pl.BlockSpec((nb, c, n), lambda i, j: (i, j, 0))
    return pl.pallas_call(
        _delta_kernel,
        grid=(b // nb, s // c),
        in_specs=[blk(kd), blk(kd), blk(kd), blk(kd), blk(LANES),
                  pl.BlockSpec((nb, 1, heads, c), lambda i, j: (i, j, 0, 0)),
                  pl.BlockSpec((1, DN_HEAD_DIM), lambda i, j: (0, 0))],
        out_specs=blk(kd),
        out_shape=jax.ShapeDtypeStruct((b, s, kd), BF16),
        scratch_shapes=[pltpu.VMEM((nb * heads, DN_HEAD_DIM, DN_HEAD_DIM), F32)],
        compiler_params=_params(("parallel", "arbitrary")),
        name="delta_rule",
    )(q, k, v, z, bg, g_rows, norm_g)


def _out_router_kernel(h_ref, o_ref, wo_ref, g_ref, rw_ref, rb_ref, h_out_ref, route_ref):
    h = h_ref[...] + _dot(o_ref[...], wo_ref[...])
    h_out_ref[...] = h
    logits = _dot_f32(_rms(h, g_ref[...]), rw_ref[...]) + rb_ref[...]
    lane = lax.broadcasted_iota(jnp.int32, logits.shape, 1)
    remaining = logits
    tops, picks = [], []
    for _ in range(TOP_K):
        top = jnp.max(remaining, axis=1, keepdims=True)
        pick = jnp.min(jnp.where(remaining == top, lane, LANES), axis=1, keepdims=True)
        tops.append(top)
        picks.append(pick)
        remaining = jnp.where(lane == pick, -jnp.inf, remaining)
    exps = [jnp.exp(top - tops[0]) for top in tops]
    denom = sum(exps)
    route = jnp.zeros_like(logits)
    for j, (pick, ex) in enumerate(zip(picks, exps)):
        route = jnp.where(lane == j, pick.astype(F32), route)
        route = jnp.where(lane == TOP_K + j, ex / denom, route)
    route_ref[...] = route


def _out_proj_router(h, o, w_out, g, router_w_pad, router_b_pad):
    t, d = h.shape
    tm = _pick_tile(t, 512, SUBLANES)
    row = lambda n: pl.BlockSpec((tm, n), lambda i: (i, 0))
    return pl.pallas_call(
        _out_router_kernel,
        grid=(t // tm,),
        in_specs=[row(d), row(o.shape[1]), _resident(w_out.shape), _resident((1, d)),
                  _resident(router_w_pad.shape), _resident((1, LANES))],
        out_specs=[row(d), row(LANES)],
        out_shape=[jax.ShapeDtypeStruct((t, d), F32), jax.ShapeDtypeStruct((t, LANES), F32)],
        compiler_params=_params(("parallel",)),
        name="out_proj_router",
    )(h, o, w_out, g, router_w_pad, router_b_pad)


def _routing_tables(route, n_experts, tile_rows):
    t = route.shape[0]
    assert (TOP_K * t) % tile_rows == 0
    idx = route[:, :TOP_K].astype(jnp.int32)
    member = (idx[:, :, None] == jnp.arange(n_experts, dtype=jnp.int32)).any(axis=1).astype(jnp.int32)
    counts = member.sum(axis=0)
    rank = jnp.cumsum(member, axis=0) - member
    padded = (counts + tile_rows - 1) // tile_rows * tile_rows
    ends = jnp.cumsum(padded)
    starts = ends - padded
    slot = jnp.take_along_axis(starts[None, :] + rank, idx, axis=1).astype(jnp.int32)
    n_tiles = TOP_K * t // tile_rows + n_experts
    n_used = (ends[-1] // tile_rows).astype(jnp.int32)
    tile_start = jnp.arange(n_tiles, dtype=jnp.int32) * tile_rows
    tile_expert = jnp.sum(tile_start[:, None] >= ends[None, :], axis=1).astype(jnp.int32)
    tile_expert = jnp.where(jnp.arange(n_tiles) < n_used, tile_expert, tile_expert[n_used - 1])
    return slot, tile_expert, n_used.reshape(1)


def _row_copy(src_ref, src_row, dst_ref, dst_row, sem):
    return pltpu.make_async_copy(src_ref.at[pl.ds(src_row, 1)], dst_ref.at[pl.ds(dst_row, 1)], sem)


def _dispatch_kernel(slot_ref, h_ref, xs_init_hbm, xs_hbm, sem):
    del xs_init_hbm
    tm = h_ref.shape[0]

    def copies(i):
        return [_row_copy(h_ref, i, xs_hbm, slot_ref[TOP_K * i + j], sem) for j in range(TOP_K)]

    @pl.loop(0, tm)
    def _(i):
        for cp in copies(i):
            cp.start()

    @pl.loop(0, tm)
    def _(i):
        for cp in copies(i):
            cp.wait()


def _dispatch(h, slot_flat, n_rows):
    t, d = h.shape
    tm = _pick_tile(t, 512, SUBLANES)
    return pl.pallas_call(
        _dispatch_kernel,
        grid=(t // tm,),
        in_specs=[pl.BlockSpec((TOP_K * tm,), lambda i: (i,), memory_space=pltpu.SMEM),
                  pl.BlockSpec((tm, d), lambda i: (i, 0)), pl.BlockSpec(memory_space=pl.ANY)],
        out_specs=pl.BlockSpec(memory_space=pl.ANY),
        out_shape=jax.ShapeDtypeStruct((n_rows, d), F32),
        scratch_shapes=[pltpu.SemaphoreType.DMA(())],
        input_output_aliases={2: 0},
        compiler_params=_params(("arbitrary",)),
        name="moe_dispatch",
    )(slot_flat, h, jnp.zeros((n_rows, d), F32))


def _grouped_ffn_kernel(te_ref, nu_ref, x_ref, g_ref, wg_ref, wu_ref, wd_ref, y_ref, xn_ref):
    del te_ref
    f = pl.program_id(1)
    used = pl.program_id(0) < nu_ref[0]

    @pl.when(jnp.logical_not(used) & (f == 0))
    def _():
        y_ref[...] = jnp.zeros_like(y_ref)

    @pl.when(used)
    def _():
        @pl.when(f == 0)
        def _():
            xn_ref[...] = _rms(x_ref[...], g_ref[...]).astype(BF16)

        xn = xn_ref[...]
        act = _silu(_dot(xn, wg_ref[0])) * _dot(xn, wu_ref[0])
        y = _dot(act.astype(BF16), wd_ref[0])

        @pl.when(f == 0)
        def _():
            y_ref[...] = y

        @pl.when(f > 0)
        def _():
            y_ref[...] += y


def _grouped_ffn(xs, g, w_gate, w_up, w_down, tile_expert, n_used, *, tile_rows, tf_target):
    n_rows, d = xs.shape
    ff = w_gate.shape[2]
    tf = _pick_tile(ff, tf_target, LANES)
    nf = ff // tf
    n_tiles = n_rows // tile_rows

    def row_map(i, f, te, nu):
        return (jnp.minimum(i, nu[0] - 1), 0)

    def ff_index(i, f, nu):
        return jnp.where(i < nu[0], f, nf - 1)

    return pl.pallas_call(
        _grouped_ffn_kernel,
        grid_spec=pltpu.PrefetchScalarGridSpec(
            num_scalar_prefetch=2,
            grid=(n_tiles, nf),
            in_specs=[
                pl.BlockSpec((tile_rows, d), row_map),
                pl.BlockSpec((1, d), lambda i, f, te, nu: (0, 0)),
                pl.BlockSpec((1, d, tf), lambda i, f, te, nu: (te[i], 0, ff_index(i, f, nu))),
                pl.BlockSpec((1, d, tf), lambda i, f, te, nu: (te[i], 0, ff_index(i, f, nu))),
                pl.BlockSpec((1, tf, d), lambda i, f, te, nu: (te[i], ff_index(i, f, nu), 0)),
            ],
            out_specs=pl.BlockSpec((tile_rows, d), lambda i, f, te, nu: (i, 0)),
            scratch_shapes=[pltpu.VMEM((tile_rows, d), BF16)],
        ),
        out_shape=jax.ShapeDtypeStruct((n_rows, d), F32),
        compiler_params=_params(("arbitrary", "arbitrary")),
        name="moe_grouped_ffn",
    )(tile_expert, n_used, xs, g, w_gate, w_up, w_down)


def _combine_kernel(slot_ref, h_ref, route_ref, gf_ref, y_hbm, o_ref, ybuf_ref, sem):
    tm = h_ref.shape[0]

    def copies(i):
        return [_row_copy(y_hbm, slot_ref[TOP_K * i + j], ybuf_ref.at[j], i, sem) for j in range(TOP_K)]

    @pl.loop(0, tm)
    def _(i):
        for cp in copies(i):
            cp.start()

    @pl.loop(0, tm)
    def _(i):
        for cp in copies(i):
            cp.wait()

    route = route_ref[...]
    out = h_ref[...]
    for j in range(TOP_K):
        out = out + route[:, TOP_K + j:TOP_K + j + 1] * ybuf_ref[j]
    o_ref[...] = _rms(out, gf_ref[...])


def _combine(h, route, slot_flat, y, final_g):
    t, d = h.shape
    tm = _pick_tile(t, 256, SUBLANES)
    row = lambda n: pl.BlockSpec((tm, n), lambda i: (i, 0))
    return pl.pallas_call(
        _combine_kernel,
        grid=(t // tm,),
        in_specs=[pl.BlockSpec((TOP_K * tm,), lambda i: (i,), memory_space=pltpu.SMEM),
                  row(d), row(LANES), pl.BlockSpec((1, d), lambda i: (0, 0)),
                  pl.BlockSpec(memory_space=pl.ANY)],
        out_specs=row(d),
        out_shape=jax.ShapeDtypeStruct((t, d), F32),
        scratch_shapes=[pltpu.VMEM((TOP_K, tm, d), F32), pltpu.SemaphoreType.DMA(())],
        compiler_params=_params(("arbitrary",)),
        name="moe_combine",
    )(slot_flat, h, route, final_g, y)


def _pad_lanes(a, offset=0, fill=0.0):
    n = a.shape[-1]
    pad = [(0, 0)] * (a.ndim - 1) + [(offset, LANES - offset - n)]
    return jnp.pad(a, pad, constant_values=fill)


def kernel(x, norm_mix_g, norm_ffn_g, pool_w_in, pool_w_group, pool_scale, dn_w_in, dn_conv_w, dn_a_log, dn_dt_bias, dn_norm_g, dn_w_out, ffn_w_gate, ffn_w_up, ffn_w_down, moe_router_w, moe_router_b, moe_w_gate, moe_w_up, moe_w_down, final_norm_g):
    b, s, d = x.shape
    t = b * s
    kd = dn_w_out.shape[1]
    heads = kd // DN_HEAD_DIM
    n_experts = moe_router_w.shape[-1]
    assert 2 * heads <= LANES and n_experts <= LANES and 2 * TOP_K <= LANES
    vec = lambda a: a.reshape(1, -1).astype(F32)

    h = x.reshape(t, d)

    h = _pool_mixer(h, s, vec(norm_mix_g[0]), pool_w_in[0].astype(BF16), pool_w_group[0].astype(BF16),
                    vec(pool_scale[0]))
    h = _ffn(h, vec(norm_ffn_g[0]), ffn_w_gate[0].astype(BF16), ffn_w_up[0].astype(BF16),
             ffn_w_down[0].astype(BF16))

    w_in = dn_w_in[0]
    w_small = _pad_lanes(w_in[:, 4 * kd:]).astype(BF16)
    q, k, v, z, bg = _dn_in_proj(
        h, s, vec(norm_mix_g[1]), w_in[:, :3 * kd].astype(BF16), w_in[:, 3 * kd:4 * kd].astype(BF16), w_small,
        dn_conv_w[0].astype(F32), _pad_lanes(vec(dn_a_log[0]), heads), _pad_lanes(vec(dn_dt_bias[0]), heads))
    g_rows = bg[:, heads:2 * heads].reshape(b, s // DN_CHUNK, DN_CHUNK, heads).transpose(0, 1, 3, 2)
    shape3 = lambda a: a.reshape(b, s, a.shape[-1])
    o = _delta_rule(shape3(q), shape3(k), shape3(v), shape3(z), shape3(bg), g_rows, vec(dn_norm_g[0]),
                    seqs_per_step=2 if b % 2 == 0 else 1)
    h, route = _out_proj_router(
        h, o.reshape(t, kd), dn_w_out[0].astype(BF16), vec(norm_ffn_g[1]),
        _pad_lanes(moe_router_w[0].astype(F32)), _pad_lanes(vec(moe_router_b[0]), fill=-jnp.inf))

    tile_rows = _pick_tile(TOP_K * t, MOE_TILE_ROWS, SUBLANES)
    slot, tile_expert, n_used = _routing_tables(route, n_experts, tile_rows)
    slot_flat = slot.reshape(-1)
    xs = _dispatch(h, slot_flat, tile_expert.shape[0] * tile_rows)
    y = _grouped_ffn(xs, vec(norm_ffn_g[1]), moe_w_gate[0].astype(BF16), moe_w_up[0].astype(BF16),
                     moe_w_down[0].astype(BF16), tile_expert, n_used, tile_rows=tile_rows, tf_target=512)
    out = _combine(h, route, slot_flat, y, vec(final_norm_g))
    return out.reshape(b, s, d)
```

```python
import functools

import jax
import jax.numpy as jnp
from jax import lax
from jax.experimental import pallas as pl
from jax.experimental.pallas import tpu as pltpu

F32 = jnp.float32
BF16 = jnp.bfloat16

RMS_EPS = 1e-6
POOL_WINDOWS = (2, 4, 8, 16)
DN_HEAD_DIM = 128
DN_CHUNK = 64
TOP_K = 2

LANES = 128
SUBLANES = 8
VMEM_LIMIT_BYTES = 56 * 1024 * 1024

POOL_HALO = 16
CONV_HALO = SUBLANES

MOE_TILE_ROWS = 512
ROW_DMA_UNROLL = 8


def _dot(a, b):
    return jnp.dot(a, b, preferred_element_type=F32)


def _dot_f32(a, b):
    return jnp.dot(a, b, preferred_element_type=F32, precision=lax.Precision.HIGHEST)


def _dot_nt(a, b):
    return lax.dot_general(a, b, (((1,), (1,)), ((), ())), preferred_element_type=F32)


def _dot_tn(a, b):
    return lax.dot_general(a, b, (((0,), (0,)), ((), ())), preferred_element_type=F32)


def _rms(x, g):
    return x * lax.rsqrt(jnp.mean(x * x, axis=-1, keepdims=True) + RMS_EPS) * g


def _silu(x):
    return x * jax.nn.sigmoid(x)


def _pick_tile(n, target, quantum):
    if n <= target:
        return n
    best = None
    for t in range(quantum, target + 1, quantum):
        if n % t == 0:
            best = t
    assert best is not None, (n, target, quantum)
    return best


def _params(semantics):
    return pltpu.CompilerParams(dimension_semantics=semantics, vmem_limit_bytes=VMEM_LIMIT_BYTES)


def _resident(shape):
    nd = len(shape)
    return pl.BlockSpec(shape, lambda *_: (0,) * nd, pipeline_mode=pl.Buffered(1))


def _pool_kernel(x_ref, xh_ref, g_ref, win_ref, wgrp_ref, sc_ref, o_ref, ext_ref, *, tiles_per_seq):
    tm = x_ref.shape[0]
    j = pl.program_id(0) % tiles_per_seq
    x = x_ref[...]
    g = g_ref[...]
    win = win_ref[...]
    u = _dot(_rms(x, g).astype(BF16), win)
    uh = _dot(_rms(xh_ref[...], g).astype(BF16), win)
    ext_ref[0:POOL_HALO, :] = jnp.where(j == 0, 0.0, uh)
    ext_ref[POOL_HALO:, :] = u
    pos = (j * tm + lax.broadcasted_iota(jnp.int32, (tm, 1), 0) + 1).astype(F32)
    c = u.shape[1] // len(POOL_WINDOWS)
    for gi, w in enumerate(POOL_WINDOWS):
        sl = slice(gi * c, (gi + 1) * c)
        s = ext_ref[:, sl]
        sh = 1
        while sh < w:
            s = s + pltpu.roll(s, sh, 0)
            sh *= 2
        m = s[POOL_HALO:] / jnp.minimum(pos, float(w)) - u[:, sl]
        y = _dot(m.astype(BF16), wgrp_ref[gi])
        o_ref[:, sl] = x[:, sl] + y * sc_ref[:, sl]


def _pool_mixer(x2d, seq, g, w_in, w_group, scale):
    t, d = x2d.shape
    assert all(w & (w - 1) == 0 and w <= POOL_HALO for w in POOL_WINDOWS)
    tm = _pick_tile(seq, 512, POOL_HALO)
    assert tm % POOL_HALO == 0
    halo_blocks = tm // POOL_HALO
    return pl.pallas_call(
        functools.partial(_pool_kernel, tiles_per_seq=seq // tm),
        grid=(t // tm,),
        in_specs=[
            pl.BlockSpec((tm, d), lambda i: (i, 0)),
            pl.BlockSpec((POOL_HALO, d), lambda i: (jnp.maximum(i * halo_blocks - 1, 0), 0)),
            _resident((1, d)),
            _resident(w_in.shape),
            _resident(w_group.shape),
            _resident((1, d)),
        ],
        out_specs=pl.BlockSpec((tm, d), lambda i: (i, 0)),
        out_shape=jax.ShapeDtypeStruct((t, d), F32),
        scratch_shapes=[pltpu.VMEM((tm + POOL_HALO, d), F32)],
        compiler_params=_params(("parallel",)),
        name="pool_mixer",
    )(x2d, x2d, g, w_in, w_group, scale)


def _ffn_kernel(h_ref, g_ref, wg_ref, wu_ref, wd_ref, o_ref):
    h = h_ref[...]
    hn = _rms(h, g_ref[...]).astype(BF16)
    act = _silu(_dot(hn, wg_ref[...])) * _dot(hn, wu_ref[...])
    o_ref[...] = h + _dot(act.astype(BF16), wd_ref[...])


def _ffn(h, g, w_gate, w_up, w_down):
    t, d = h.shape
    tm = _pick_tile(t, 512, SUBLANES)
    row = pl.BlockSpec((tm, d), lambda i: (i, 0))
    return pl.pallas_call(
        _ffn_kernel,
        grid=(t // tm,),
        in_specs=[row, _resident((1, d)), _resident(w_gate.shape), _resident(w_up.shape), _resident(w_down.shape)],
        out_specs=row,
        out_shape=jax.ShapeDtypeStruct((t, d), F32),
        compiler_params=_params(("parallel",)),
        name="dense_ffn",
    )(h, g, w_gate, w_up, w_down)


def _dn_in_kernel(h_ref, hh_ref, g_ref, wqkv_ref, wz_ref, wsm_ref, cw_ref, alog_ref, dtb_ref,
                  q_ref, k_ref, v_ref, z_ref, bg_ref, ext_ref, *, tiles_per_seq):
    j = pl.program_id(0) % tiles_per_seq
    g = g_ref[...]
    hn = _rms(h_ref[...], g).astype(BF16)
    hnh = jnp.where(j == 0, 0.0, _rms(hh_ref[...], g)).astype(BF16)
    kd = q_ref.shape[1]
    heads = kd // DN_HEAD_DIM
    taps = cw_ref.shape[0]
    for part, out_ref in enumerate((q_ref, k_ref, v_ref)):
        cols = slice(part * kd, (part + 1) * kd)
        w = wqkv_ref[:, cols]
        ext_ref[0:CONV_HALO, :] = _dot(hnh, w)
        ext_ref[CONV_HALO:, :] = _dot(hn, w)
        p = ext_ref[...]
        cw = cw_ref[:, cols]
        y = p * cw[taps - 1:taps, :]
        for s in range(1, taps):
            y = y + pltpu.roll(p, s, 0) * cw[taps - 1 - s:taps - s, :]
        y = _silu(y[CONV_HALO:])
        if part == 2:
            out_ref[...] = y
            continue
        for h in range(heads):
            hs = slice(h * DN_HEAD_DIM, (h + 1) * DN_HEAD_DIM)
            yh = y[:, hs]
            yh = yh * lax.rsqrt(jnp.sum(yh * yh, axis=-1, keepdims=True) + RMS_EPS)
            if part == 0:
                yh = yh * (DN_HEAD_DIM ** -0.5)
            out_ref[:, hs] = yh
    z_ref[...] = _dot(hn, wz_ref[...])
    sm = _dot(hn, wsm_ref[...])
    xs = sm + dtb_ref[...]
    softplus = jnp.maximum(xs, 0.0) + jnp.log1p(jnp.exp(-jnp.abs(xs)))
    lane = lax.broadcasted_iota(jnp.int32, sm.shape, 1)
    bg_ref[...] = jnp.where(lane < heads, jax.nn.sigmoid(sm), -jnp.exp(alog_ref[...]) * softplus)


def _dn_in_proj(h, seq, g, w_qkv, w_z, w_small, conv_w, a_log_pad, dt_bias_pad):
    t, d = h.shape
    kd = w_z.shape[1]
    assert conv_w.shape[0] - 1 <= CONV_HALO
    tm = _pick_tile(seq, 512, CONV_HALO)
    halo_blocks = tm // CONV_HALO
    row = lambda n: pl.BlockSpec((tm, n), lambda i: (i, 0))
    return pl.pallas_call(
        functools.partial(_dn_in_kernel, tiles_per_seq=seq // tm),
        grid=(t // tm,),
        in_specs=[
            row(d),
            pl.BlockSpec((CONV_HALO, d), lambda i: (jnp.maximum(i * halo_blocks - 1, 0), 0)),
            _resident((1, d)),
            _resident(w_qkv.shape),
            _resident(w_z.shape),
            _resident(w_small.shape),
            _resident(conv_w.shape),
            _resident((1, LANES)),
            _resident((1, LANES)),
        ],
        out_specs=[row(kd), row(kd), row(kd), row(kd), row(LANES)],
        out_shape=[jax.ShapeDtypeStruct((t, kd), F32)] * 4 + [jax.ShapeDtypeStruct((t, LANES), F32)],
        scratch_shapes=[pltpu.VMEM((tm + CONV_HALO, kd), F32)],
        compiler_params=_params(("parallel",)),
        name="deltanet_in_proj",
    )(h, h, g, w_qkv, w_z, w_small, conv_w, a_log_pad, dt_bias_pad)


def _delta_kernel(q_ref, k_ref, v_ref, z_ref, bg_ref, gt_ref, ng_ref, o_ref, state_ref):
    nb, c, kd = q_ref.shape
    heads = kd // DN_HEAD_DIM
    pairs = [(b, h) for b in range(nb) for h in range(heads)]

    @pl.when(pl.program_id(1) == 0)
    def _():
        state_ref[...] = jnp.zeros_like(state_ref)

    def head_cols(h):
        return slice(h * DN_HEAD_DIM, (h + 1) * DN_HEAD_DIM)

    row = lax.broadcasted_iota(jnp.int32, (c, c), 0)
    col = lax.broadcasted_iota(jnp.int32, (c, c), 1)
    causal = row >= col
    strict = row > col
    diag = row == col
    lower_ones = causal.astype(F32)
    upper_ones = (row <= col).astype(F32)
    bg = [bg_ref[b] for b in range(nb)]
    g_cum_col = [_dot_f32(lower_ones, bg[b]) for b in range(nb)]
    g_cum_row = [_dot_f32(gt_ref[b, 0], upper_ones) for b in range(nb)]

    q = [q_ref[b, :, head_cols(h)] for b, h in pairs]
    k = [k_ref[b, :, head_cols(h)] for b, h in pairs]
    beta = [bg[b][:, h:h + 1] for b, h in pairs]
    gc = [g_cum_col[b][:, heads + h:heads + h + 1] for b, h in pairs]
    gr = [g_cum_row[b][h:h + 1, :] for b, h in pairs]
    n = len(pairs)
    rng = range(n)
    decay = [jnp.where(causal, jnp.exp(jnp.where(causal, gc[i] - gr[i], 0.0)), 0.0) for i in rng]
    k_beta = [k[i] * beta[i] for i in rng]
    scores = [_dot_nt(jnp.concatenate([k_beta[i], q[i]], axis=0).astype(BF16), k[i].astype(BF16))
              for i in rng]
    qk = [(scores[i][c:] * decay[i]).astype(BF16) for i in rng]
    x = [jnp.where(strict, -(scores[i][:c] * decay[i]), 0.0) for i in rng]
    inv = [jnp.where(diag, 1.0, x[i]) for i in rng]
    p = 1
    while 2 * p < c:
        x16 = [x[i].astype(BF16) for i in rng]
        x = [_dot(x16[i], x16[i]) for i in rng]
        inv = [inv[i] + _dot(inv[i].astype(BF16), x[i].astype(BF16)) for i in rng]
        p *= 2
    e_gc = [jnp.exp(gc[i]) for i in rng]
    rhs = [jnp.concatenate([v_ref[b, :, head_cols(h)] * beta[i], k_beta[i] * e_gc[i]], axis=1).astype(BF16)
           for i, (b, h) in enumerate(pairs)]
    uw = [_dot(inv[i].astype(BF16), rhs[i]) for i in rng]
    state = [state_ref[i] for i in rng]
    ws_qs = [_dot(jnp.concatenate([uw[i][:, DN_HEAD_DIM:], q[i] * e_gc[i]], axis=0).astype(BF16),
                  state[i].astype(BF16)) for i in rng]
    v_new = [(uw[i][:, :DN_HEAD_DIM] - ws_qs[i][:c]).astype(BF16) for i in rng]
    o = [ws_qs[i][c:] + _dot(qk[i], v_new[i]) for i in rng]
    g_last = [gc[i][c - 1:c, :] for i in rng]
    k_dec = [(k[i] * jnp.exp(g_last[i] - gc[i])).astype(BF16) for i in rng]
    for i in rng:
        state_ref[i] = state[i] * jnp.exp(g_last[i]) + _dot_tn(k_dec[i], v_new[i])
    ng = ng_ref[...]
    for i, (b, h) in enumerate(pairs):
        on = o[i] * lax.rsqrt(jnp.mean(o[i] * o[i], axis=-1, keepdims=True) + RMS_EPS) * ng
        o_ref[b, :, head_cols(h)] = (on * _silu(z_ref[b, :, head_cols(h)])).astype(BF16)


def _delta_rule(q, k, v, z, bg, g_rows, norm_g, *, seqs_per_step):
    b, s, kd = q.shape
    heads = kd // DN_HEAD_DIM
    c = DN_CHUNK
    nb = seqs_per_step
    assert s % c == 0 and c & (c - 1) == 0 and b % nb == 0
    blk = lambda n: pl.BlockSpec((nb, c, n), lambda i, j: (i, j, 0))
    return pl.pallas_call(
        _delta_kernel,
        grid=(b // nb, s // c),
        in_specs=[blk(kd), blk(kd), blk(kd), blk(kd), blk(LANES),
                  pl.BlockSpec((nb, 1, heads, c), lambda i, j: (i, j, 0, 0)),
                  pl.BlockSpec((1, DN_HEAD_DIM), lambda i, j: (0, 0))],
        out_specs=blk(kd),
        out_shape=jax.ShapeDtypeStruct((b, s, kd), BF16),
        scratch_shapes=[pltpu.VMEM((nb * heads, DN_HEAD_DIM, DN_HEAD_DIM), F32)],
        compiler_params=_params(("parallel", "arbitrary")),
        name="delta_rule",
    )(q, k, v, z, bg, g_rows, norm_g)


def _split_bf16(a):
    hi = a.astype(BF16)
    return hi, (a - hi.astype(F32)).astype(BF16)


def _out_router_kernel(h_ref, o_ref, wo_ref, g_ref, rw_hi_ref, rw_lo_ref, rb_ref, h_out_ref, route_ref):
    h = h_ref[...] + _dot(o_ref[...], wo_ref[...])
    h_out_ref[...] = h
    hn_hi, hn_lo = _split_bf16(_rms(h, g_ref[...]))
    rw_hi = rw_hi_ref[...]
    logits = _dot(hn_hi, rw_hi) + (_dot(hn_lo, rw_hi) + _dot(hn_hi, rw_lo_ref[...]))
    logits = logits + rb_ref[...]
    lane = lax.broadcasted_iota(jnp.int32, logits.shape, 1)
    remaining = logits
    tops, picks = [], []
    for _ in range(TOP_K):
        top = jnp.max(remaining, axis=1, keepdims=True)
        pick = jnp.min(jnp.where(remaining == top, lane, LANES), axis=1, keepdims=True)
        tops.append(top)
        picks.append(pick)
        remaining = jnp.where(lane == pick, -jnp.inf, remaining)
    exps = [jnp.exp(top - tops[0]) for top in tops]
    denom = sum(exps)
    route = jnp.zeros_like(logits)
    for j, (pick, ex) in enumerate(zip(picks, exps)):
        route = jnp.where(lane == j, pick.astype(F32), route)
        route = jnp.where(lane == TOP_K + j, ex / denom, route)
    route_ref[...] = route


def _out_proj_router(h, o, w_out, g, router_w_pad, router_b_pad):
    rw_hi, rw_lo = _split_bf16(router_w_pad)
    t, d = h.shape
    tm = _pick_tile(t, 512, SUBLANES)
    row = lambda n: pl.BlockSpec((tm, n), lambda i: (i, 0))
    return pl.pallas_call(
        _out_router_kernel,
        grid=(t // tm,),
        in_specs=[row(d), row(o.shape[1]), _resident(w_out.shape), _resident((1, d)),
                  _resident(router_w_pad.shape), _resident(router_w_pad.shape), _resident((1, LANES))],
        out_specs=[row(d), row(LANES)],
        out_shape=[jax.ShapeDtypeStruct((t, d), F32), jax.ShapeDtypeStruct((t, LANES), F32)],
        compiler_params=_params(("parallel",)),
        name="out_proj_router",
    )(h, o, w_out, g, rw_hi, rw_lo, router_b_pad)


def _routing_tables(route, n_experts, tile_rows):
    t = route.shape[0]
    assert (TOP_K * t) % tile_rows == 0
    idx = route[:, :TOP_K].astype(jnp.int32)
    member = (idx[:, :, None] == jnp.arange(n_experts, dtype=jnp.int32)).any(axis=1).astype(jnp.int32)
    counts = member.sum(axis=0)
    rank = jnp.cumsum(member, axis=0) - member
    padded = (counts + tile_rows - 1) // tile_rows * tile_rows
    ends = jnp.cumsum(padded)
    starts = ends - padded
    slot = jnp.take_along_axis(starts[None, :] + rank, idx, axis=1).astype(jnp.int32)
    n_tiles = TOP_K * t // tile_rows + n_experts
    n_used = (ends[-1] // tile_rows).astype(jnp.int32)
    tile_start = jnp.arange(n_tiles, dtype=jnp.int32) * tile_rows
    tile_expert = jnp.sum(tile_start[:, None] >= ends[None, :], axis=1).astype(jnp.int32)
    tile_expert = jnp.where(jnp.arange(n_tiles) < n_used, tile_expert, tile_expert[n_used - 1])
    last_tile = jnp.where(padded > 0, ends // tile_rows - 1, -1)
    trailing = n_used + jnp.arange(n_experts, dtype=jnp.int32)
    trailing = jnp.where(trailing < n_tiles, trailing, -1)
    zero_tiles = jnp.concatenate([last_tile, trailing]).astype(jnp.int32)
    return slot, tile_expert, n_used.reshape(1), zero_tiles


def _row_copy(src_ref, src_row, dst_ref, dst_row, sem):
    return pltpu.make_async_copy(src_ref.at[pl.ds(src_row, 1)], dst_ref.at[pl.ds(dst_row, 1)], sem)


def _dispatch_kernel(zero_tiles_ref, slot_ref, h_ref, xs_hbm, zeros_ref, sem):
    tm = h_ref.shape[0]
    tile_rows = zeros_ref.shape[0]

    @pl.when(pl.program_id(0) == 0)
    def _():
        zeros_ref[...] = jnp.zeros_like(zeros_ref)

        def fill(j):
            row = pl.multiple_of(jnp.maximum(zero_tiles_ref[j], 0) * tile_rows, tile_rows)
            return pltpu.make_async_copy(zeros_ref, xs_hbm.at[pl.ds(row, tile_rows)], sem)

        for j in range(zero_tiles_ref.shape[0]):
            pl.when(zero_tiles_ref[j] >= 0)(fill(j).start)
        for j in range(zero_tiles_ref.shape[0]):
            pl.when(zero_tiles_ref[j] >= 0)(fill(j).wait)

    @pl.loop(0, tm, unroll=ROW_DMA_UNROLL)
    def _(i):
        for j in range(TOP_K):
            _row_copy(h_ref, i, xs_hbm, slot_ref[TOP_K * i + j], sem).start()

    for _ in range(TOP_K):
        pltpu.make_async_copy(h_ref, xs_hbm.at[pl.ds(0, tm)], sem).wait()


def _dispatch(h, slot_flat, zero_tiles, n_rows, tile_rows):
    t, d = h.shape
    tm = _pick_tile(t, 512, SUBLANES)
    return pl.pallas_call(
        _dispatch_kernel,
        grid=(t // tm,),
        in_specs=[pl.BlockSpec(zero_tiles.shape, lambda i: (0,), memory_space=pltpu.SMEM),
                  pl.BlockSpec((TOP_K * tm,), lambda i: (i,), memory_space=pltpu.SMEM),
                  pl.BlockSpec((tm, d), lambda i: (i, 0))],
        out_specs=pl.BlockSpec(memory_space=pl.ANY),
        out_shape=jax.ShapeDtypeStruct((n_rows, d), F32),
        scratch_shapes=[pltpu.VMEM((tile_rows, d), F32), pltpu.SemaphoreType.DMA(())],
        compiler_params=_params(("arbitrary",)),
        name="moe_dispatch",
    )(zero_tiles, slot_flat, h)


def _grouped_ffn_kernel(te_ref, nu_ref, x_ref, g_ref, wg_ref, wu_ref, wd_ref, y_ref):
    del te_ref
    used = pl.program_id(0) < nu_ref[0]

    @pl.when(jnp.logical_not(used))
    def _():
        y_ref[...] = jnp.zeros_like(y_ref)

    @pl.when(used)
    def _():
        xn = _rms(x_ref[...], g_ref[...]).astype(BF16)
        act = _silu(_dot(xn, wg_ref[0])) * _dot(xn, wu_ref[0])
        y_ref[...] = _dot(act.astype(BF16), wd_ref[0])


def _grouped_ffn(xs, g, w_gate, w_up, w_down, tile_expert, n_used, *, tile_rows):
    n_rows, d = xs.shape
    ff = w_gate.shape[2]

    def x_map(i, te, nu):
        return (jnp.minimum(i, nu[0] - 1), 0)

    def expert_block(shape):
        return pl.BlockSpec((1,) + shape, lambda i, te, nu: (te[i], 0, 0), pipeline_mode=pl.Buffered(1))

    return pl.pallas_call(
        _grouped_ffn_kernel,
        grid_spec=pltpu.PrefetchScalarGridSpec(
            num_scalar_prefetch=2,
            grid=(n_rows // tile_rows,),
            in_specs=[
                pl.BlockSpec((tile_rows, d), x_map),
                pl.BlockSpec((1, d), lambda i, te, nu: (0, 0)),
                expert_block((d, ff)),
                expert_block((d, ff)),
                expert_block((ff, d)),
            ],
            out_specs=pl.BlockSpec((tile_rows, d), lambda i, te, nu: (i, 0)),
        ),
        out_shape=jax.ShapeDtypeStruct((n_rows, d), F32),
        compiler_params=_params(("arbitrary",)),
        name="moe_grouped_ffn",
    )(tile_expert, n_used, xs, g, w_gate, w_up, w_down)


def _combine_kernel(slot_ref, h_ref, route_ref, gf_ref, y_hbm, o_ref, ybuf_ref, sem):
    tm = h_ref.shape[0]

    @pl.loop(0, tm, unroll=ROW_DMA_UNROLL)
    def _(i):
        for j in range(TOP_K):
            _row_copy(y_hbm, slot_ref[TOP_K * i + j], ybuf_ref.at[j], i, sem).start()

    for j in range(TOP_K):
        pltpu.make_async_copy(y_hbm.at[pl.ds(0, tm)], ybuf_ref.at[j], sem).wait()

    route = route_ref[...]
    out = h_ref[...]
    for j in range(TOP_K):
        out = out + route[:, TOP_K + j:TOP_K + j + 1] * ybuf_ref[j]
    o_ref[...] = _rms(out, gf_ref[...])


def _combine(h, route, slot_flat, y, final_g):
    t, d = h.shape
    tm = _pick_tile(t, 256, SUBLANES)
    row = lambda n: pl.BlockSpec((tm, n), lambda i: (i, 0))
    return pl.pallas_call(
        _combine_kernel,
        grid=(t // tm,),
        in_specs=[pl.BlockSpec((TOP_K * tm,), lambda i: (i,), memory_space=pltpu.SMEM),
                  row(d), row(LANES), pl.BlockSpec((1, d), lambda i: (0, 0)),
                  pl.BlockSpec(memory_space=pl.ANY)],
        out_specs=row(d),
        out_shape=jax.ShapeDtypeStruct((t, d), F32),
        scratch_shapes=[pltpu.VMEM((TOP_K, tm, d), F32), pltpu.SemaphoreType.DMA(())],
        compiler_params=_params(("arbitrary",)),
        name="moe_combine",
    )(slot_flat, h, route, final_g, y)


def _pad_lanes(a, offset=0, fill=0.0):
    n = a.shape[-1]
    pad = [(0, 0)] * (a.ndim - 1) + [(offset, LANES - offset - n)]
    return jnp.pad(a, pad, constant_values=fill)


def kernel(x, norm_mix_g, norm_ffn_g, pool_w_in, pool_w_group, pool_scale, dn_w_in, dn_conv_w, dn_a_log, dn_dt_bias, dn_norm_g, dn_w_out, ffn_w_gate, ffn_w_up, ffn_w_down, moe_router_w, moe_router_b, moe_w_gate, moe_w_up, moe_w_down, final_norm_g):
    b, s, d = x.shape
    t = b * s
    kd = dn_w_out.shape[1]
    heads = kd // DN_HEAD_DIM
    n_experts = moe_router_w.shape[-1]
    assert 2 * heads <= LANES and n_experts <= LANES and 2 * TOP_K <= LANES
    vec = lambda a: a.reshape(1, -1).astype(F32)

    h = x.reshape(t, d)

    h = _pool_mixer(h, s, vec(norm_mix_g[0]), pool_w_in[0].astype(BF16), pool_w_group[0].astype(BF16),
                    vec(pool_scale[0]))
    h = _ffn(h, vec(norm_ffn_g[0]), ffn_w_gate[0].astype(BF16), ffn_w_up[0].astype(BF16),
             ffn_w_down[0].astype(BF16))

    w_in = dn_w_in[0]
    w_small = _pad_lanes(w_in[:, 4 * kd:]).astype(BF16)
    q, k, v, z, bg = _dn_in_proj(
        h, s, vec(norm_mix_g[1]), w_in[:, :3 * kd].astype(BF16), w_in[:, 3 * kd:4 * kd].astype(BF16), w_small,
        dn_conv_w[0].astype(F32), _pad_lanes(vec(dn_a_log[0]), heads), _pad_lanes(vec(dn_dt_bias[0]), heads))
    g_rows = bg[:, heads:2 * heads].reshape(b, s // DN_CHUNK, DN_CHUNK, heads).transpose(0, 1, 3, 2)
    shape3 = lambda a: a.reshape(b, s, a.shape[-1])
    o = _delta_rule(shape3(q), shape3(k), shape3(v), shape3(z), shape3(bg), g_rows, vec(dn_norm_g[0]),
                    seqs_per_step=4 if b % 4 == 0 else (2 if b % 2 == 0 else 1))
    h, route = _out_proj_router(
        h, o.reshape(t, kd), dn_w_out[0].astype(BF16), vec(norm_ffn_g[1]),
        _pad_lanes(moe_router_w[0].astype(F32)), _pad_lanes(vec(moe_router_b[0]), fill=-jnp.inf))

    tile_rows = _pick_tile(TOP_K * t, MOE_TILE_ROWS, SUBLANES)
    slot, tile_expert, n_used, zero_tiles = _routing_tables(route, n_experts, tile_rows)
    slot_flat = slot.reshape(-1)
    xs = _dispatch(h, slot_flat, zero_tiles, tile_expert.shape[0] * tile_rows, tile_rows)
    y = _grouped_ffn(xs, vec(norm_ffn_g[1]), moe_w_gate[0].astype(BF16), moe_w_up[0].astype(BF16),
                     moe_w_down[0].astype(BF16), tile_expert, n_used, tile_rows=tile_rows)
    out = _combine(h, route, slot_flat, y, vec(final_norm_g))
    return out.reshape(b, s, d)
```

```python
import functools

import jax
import jax.numpy as jnp
from jax import lax
from jax.experimental import pallas as pl
from jax.experimental.pallas import tpu as pltpu

F32 = jnp.float32
BF16 = jnp.bfloat16

RMS_EPS = 1e-6
POOL_WINDOWS = (2, 4, 8, 16)
DN_HEAD_DIM = 128
DN_CHUNK = 64
TOP_K = 2

LANES = 128
SUBLANES = 8
VMEM_LIMIT_BYTES = 56 * 1024 * 1024

POOL_HALO = 16
CONV_HALO = SUBLANES

MOE_TILE_ROWS = 512
ROW_DMA_UNROLL = 8
RECORD_ROWS = 16


def _dot(a, b):
    return jnp.dot(a, b, preferred_element_type=F32)


def _dot_f32(a, b):
    return jnp.dot(a, b, preferred_element_type=F32, precision=lax.Precision.HIGHEST)


def _dot_nt(a, b):
    return lax.dot_general(a, b, (((1,), (1,)), ((), ())), preferred_element_type=F32)


def _dot_tn(a, b):
    return lax.dot_general(a, b, (((0,), (0,)), ((), ())), preferred_element_type=F32)


def _rms(x, g):
    return x * lax.rsqrt(jnp.mean(x * x, axis=-1, keepdims=True) + RMS_EPS) * g


def _silu(x):
    return x * jax.nn.sigmoid(x)


def _pick_tile(n, target, quantum):
    if n <= target:
        return n
    best = None
    for t in range(quantum, target + 1, quantum):
        if n % t == 0:
            best = t
    assert best is not None, (n, target, quantum)
    return best


def _params(semantics):
    return pltpu.CompilerParams(dimension_semantics=semantics, vmem_limit_bytes=VMEM_LIMIT_BYTES)


def _resident(shape):
    nd = len(shape)
    return pl.BlockSpec(shape, lambda *_: (0,) * nd, pipeline_mode=pl.Buffered(1))


def _layer0_kernel(x_ref, xh_ref, gm_ref, win_ref, wgrp_ref, sc_ref, gf_ref, wg_ref, wu_ref, wd_ref,
                   o_ref, ext_ref, h_ref, *, tiles_per_seq):
    tm = x_ref.shape[0]
    j = pl.program_id(0) % tiles_per_seq
    x = x_ref[...]
    gm = gm_ref[...]
    win = win_ref[...]
    u = _dot(_rms(x, gm).astype(BF16), win)
    uh = _dot(_rms(xh_ref[...], gm).astype(BF16), win)
    ext_ref[0:POOL_HALO, :] = jnp.where(j == 0, 0.0, uh)
    ext_ref[POOL_HALO:, :] = u
    pos = (j * tm + lax.broadcasted_iota(jnp.int32, (tm, 1), 0) + 1).astype(F32)
    c = u.shape[1] // len(POOL_WINDOWS)
    for gi, w in enumerate(POOL_WINDOWS):
        sl = slice(gi * c, (gi + 1) * c)
        s = ext_ref[:, sl]
        sh = 1
        while sh < w:
            s = s + pltpu.roll(s, sh, 0)
            sh *= 2
        m = s[POOL_HALO:] / jnp.minimum(pos, float(w)) - u[:, sl]
        y = _dot(m.astype(BF16), wgrp_ref[gi])
        h_ref[:, sl] = x[:, sl] + y * sc_ref[:, sl]
    h = h_ref[...]
    hn = _rms(h, gf_ref[...]).astype(BF16)
    act = _silu(_dot(hn, wg_ref[...])) * _dot(hn, wu_ref[...])
    o_ref[...] = h + _dot(act.astype(BF16), wd_ref[...])


def _layer0(x2d, seq, g_mix, w_in, w_group, scale, g_ffn, w_gate, w_up, w_down):
    t, d = x2d.shape
    assert all(w & (w - 1) == 0 and w <= POOL_HALO for w in POOL_WINDOWS)
    tm = _pick_tile(seq, 512, POOL_HALO)
    assert tm % POOL_HALO == 0
    halo_blocks = tm // POOL_HALO
    row = pl.BlockSpec((tm, d), lambda i: (i, 0))
    return pl.pallas_call(
        functools.partial(_layer0_kernel, tiles_per_seq=seq // tm),
        grid=(t // tm,),
        in_specs=[
            row,
            pl.BlockSpec((POOL_HALO, d), lambda i: (jnp.maximum(i * halo_blocks - 1, 0), 0)),
            _resident((1, d)), _resident(w_in.shape), _resident(w_group.shape), _resident((1, d)),
            _resident((1, d)), _resident(w_gate.shape), _resident(w_up.shape), _resident(w_down.shape),
        ],
        out_specs=row,
        out_shape=jax.ShapeDtypeStruct((t, d), F32),
        scratch_shapes=[pltpu.VMEM((tm + POOL_HALO, d), F32), pltpu.VMEM((tm, d), F32)],
        compiler_params=_params(("parallel",)),
        name="layer0_pool_ffn",
    )(x2d, x2d, g_mix, w_in, w_group, scale, g_ffn, w_gate, w_up, w_down)


def _dn_in_kernel(h_ref, hh_ref, g_ref, wqkv_ref, wz_ref, wsm_ref, cw_ref, alog_ref, dtb_ref,
                  q_ref, k_ref, v_ref, z_ref, bg_ref, bgt_ref, ext_ref, *, tiles_per_seq):
    j = pl.program_id(0) % tiles_per_seq
    g = g_ref[...]
    hn = _rms(h_ref[...], g).astype(BF16)
    hnh = jnp.where(j == 0, 0.0, _rms(hh_ref[...], g)).astype(BF16)
    kd = q_ref.shape[1]
    heads = kd // DN_HEAD_DIM
    taps = cw_ref.shape[0]
    for part, out_ref in enumerate((q_ref, k_ref, v_ref)):
        cols = slice(part * kd, (part + 1) * kd)
        w = wqkv_ref[:, cols]
        ext_ref[0:CONV_HALO, :] = _dot(hnh, w)
        ext_ref[CONV_HALO:, :] = _dot(hn, w)
        p = ext_ref[...]
        cw = cw_ref[:, cols]
        y = p * cw[taps - 1:taps, :]
        for s in range(1, taps):
            y = y + pltpu.roll(p, s, 0) * cw[taps - 1 - s:taps - s, :]
        y = _silu(y[CONV_HALO:])
        if part == 2:
            out_ref[...] = y
            continue
        for h in range(heads):
            hs = slice(h * DN_HEAD_DIM, (h + 1) * DN_HEAD_DIM)
            yh = y[:, hs]
            yh = yh * lax.rsqrt(jnp.sum(yh * yh, axis=-1, keepdims=True) + RMS_EPS)
            if part == 0:
                yh = yh * (DN_HEAD_DIM ** -0.5)
            out_ref[:, hs] = yh
    z_ref[...] = _dot(hn, wz_ref[...])
    sm = _dot(hn, wsm_ref[...])
    xs = sm + dtb_ref[...]
    softplus = jnp.maximum(xs, 0.0) + jnp.log1p(jnp.exp(-jnp.abs(xs)))
    lane = lax.broadcasted_iota(jnp.int32, sm.shape, 1)
    bg = jnp.where(lane < heads, jax.nn.sigmoid(sm), -jnp.exp(alog_ref[...]) * softplus)
    bg_ref[...] = bg
    bgt_ref[...] = bg.T[:bgt_ref.shape[0]]


def _dn_in_proj(h, seq, g, w_qkv, w_z, w_small, conv_w, a_log_pad, dt_bias_pad):
    t, d = h.shape
    kd = w_z.shape[1]
    assert conv_w.shape[0] - 1 <= CONV_HALO
    tm = _pick_tile(seq, 512, CONV_HALO)
    halo_blocks = tm // CONV_HALO
    row = lambda n: pl.BlockSpec((tm, n), lambda i: (i, 0))
    return pl.pallas_call(
        functools.partial(_dn_in_kernel, tiles_per_seq=seq // tm),
        grid=(t // tm,),
        in_specs=[
            row(d),
            pl.BlockSpec((CONV_HALO, d), lambda i: (jnp.maximum(i * halo_blocks - 1, 0), 0)),
            _resident((1, d)),
            _resident(w_qkv.shape),
            _resident(w_z.shape),
            _resident(w_small.shape),
            _resident(conv_w.shape),
            _resident((1, LANES)),
            _resident((1, LANES)),
        ],
        out_specs=[row(kd), row(kd), row(kd), row(kd), row(LANES),
                   pl.BlockSpec((RECORD_ROWS, tm), lambda i: (0, i))],
        out_shape=[jax.ShapeDtypeStruct((t, kd), F32)] * 4 + [jax.ShapeDtypeStruct((t, LANES), F32),
                                                              jax.ShapeDtypeStruct((RECORD_ROWS, t), F32)],
        scratch_shapes=[pltpu.VMEM((tm + CONV_HALO, kd), F32)],
        compiler_params=_params(("parallel",)),
        name="deltanet_in_proj",
    )(h, h, g, w_qkv, w_z, w_small, conv_w, a_log_pad, dt_bias_pad)


def _delta_kernel(q_ref, k_ref, v_ref, z_ref, bg_ref, gt_ref, ng_ref, o_ref, state_ref):
    nb, c, kd = q_ref.shape
    heads = kd // DN_HEAD_DIM
    pairs = [(b, h) for b in range(nb) for h in range(heads)]

    @pl.when(pl.program_id(1) == 0)
    def _():
        state_ref[...] = jnp.zeros_like(state_ref)

    def head_cols(h):
        return slice(h * DN_HEAD_DIM, (h + 1) * DN_HEAD_DIM)

    row = lax.broadcasted_iota(jnp.int32, (c, c), 0)
    col = lax.broadcasted_iota(jnp.int32, (c, c), 1)
    causal = row >= col
    strict = row > col
    diag = row == col
    lower_ones = causal.astype(F32)
    upper_ones = (row <= col).astype(F32)
    bg = [bg_ref[b] for b in range(nb)]
    g_cum_col = [_dot_f32(lower_ones, bg[b]) for b in range(nb)]
    g_cum_row = [_dot_f32(gt_ref[b, 0], upper_ones) for b in range(nb)]

    q = [q_ref[b, :, head_cols(h)] for b, h in pairs]
    k = [k_ref[b, :, head_cols(h)] for b, h in pairs]
    beta = [bg[b][:, h:h + 1] for b, h in pairs]
    gc = [g_cum_col[b][:, heads + h:heads + h + 1] for b, h in pairs]
    gr = [g_cum_row[b][h:h + 1, :] for b, h in pairs]
    n = len(pairs)
    rng = range(n)
    decay = [jnp.where(causal, jnp.exp(jnp.where(causal, gc[i] - gr[i], 0.0)), 0.0) for i in rng]
    k_beta = [k[i] * beta[i] for i in rng]
    scores = [_dot_nt(jnp.concatenate([k_beta[i], q[i]], axis=0).astype(BF16), k[i].astype(BF16))
              for i in rng]
    qk = [(scores[i][c:] * decay[i]).astype(BF16) for i in rng]
    x = [jnp.where(strict, -(scores[i][:c] * decay[i]), 0.0) for i in rng]
    inv = [jnp.where(diag, 1.0, x[i]) for i in rng]
    p = 1
    while 2 * p < c:
        x16 = [x[i].astype(BF16) for i in rng]
        x = [_dot(x16[i], x16[i]) for i in rng]
        inv = [inv[i] + _dot(inv[i].astype(BF16), x[i].astype(BF16)) for i in rng]
        p *= 2
    e_gc = [jnp.exp(gc[i]) for i in rng]
    rhs = [jnp.concatenate([v_ref[b, :, head_cols(h)] * beta[i], k_beta[i] * e_gc[i]], axis=1).astype(BF16)
           for i, (b, h) in enumerate(pairs)]
    uw = [_dot(inv[i].astype(BF16), rhs[i]) for i in rng]
    state = [state_ref[i] for i in rng]
    ws_qs = [_dot(jnp.concatenate([uw[i][:, DN_HEAD_DIM:], q[i] * e_gc[i]], axis=0).astype(BF16),
                  state[i].astype(BF16)) for i in rng]
    v_new = [(uw[i][:, :DN_HEAD_DIM] - ws_qs[i][:c]).astype(BF16) for i in rng]
    o = [ws_qs[i][c:] + _dot(qk[i], v_new[i]) for i in rng]
    g_last = [gc[i][c - 1:c, :] for i in rng]
    k_dec = [(k[i] * jnp.exp(g_last[i] - gc[i])).astype(BF16) for i in rng]
    for i in rng:
        state_ref[i] = state[i] * jnp.exp(g_last[i]) + _dot_tn(k_dec[i], v_new[i])
    ng = ng_ref[...]
    for i, (b, h) in enumerate(pairs):
        on = o[i] * lax.rsqrt(jnp.mean(o[i] * o[i], axis=-1, keepdims=True) + RMS_EPS) * ng
        o_ref[b, :, head_cols(h)] = (on * _silu(z_ref[b, :, head_cols(h)])).astype(BF16)


def _delta_rule(q, k, v, z, bg, g_rows, norm_g, *, seqs_per_step):
    b, s, kd = q.shape
    heads = kd // DN_HEAD_DIM
    c = DN_CHUNK
    nb = seqs_per_step
    assert s % c == 0 and c & (c - 1) == 0 and b % nb == 0
    blk = lambda n: pl.BlockSpec((nb, c, n), lambda i, j: (i, j, 0))
    return pl.pallas_call(
        _delta_kernel,
        grid=(b // nb, s // c),
        in_specs=[blk(kd), blk(kd), blk(kd), blk(kd), blk(LANES),
                  pl.BlockSpec((nb, 1, heads, c), lambda i, j: (i, j, 0, 0)),
                  pl.BlockSpec((1, DN_HEAD_DIM), lambda i, j: (0, 0))],
        out_specs=blk(kd),
        out_shape=jax.ShapeDtypeStruct((b, s, kd), BF16),
        scratch_shapes=[pltpu.VMEM((nb * heads, DN_HEAD_DIM, DN_HEAD_DIM), F32)],
        compiler_params=_params(("parallel", "arbitrary")),
        name="delta_rule",
    )(q, k, v, z, bg, g_rows, norm_g)


def _split_bf16(a):
    hi = a.astype(BF16)
    return hi, (a - hi.astype(F32)).astype(BF16)


def _out_router_kernel(h_ref, o_ref, wo_ref, g_ref, rw_hi_ref, rw_lo_ref, rb_ref, h_out_ref, route_ref, route_t_ref):
    h = h_ref[...] + _dot(o_ref[...], wo_ref[...])
    h_out_ref[...] = h
    hn_hi, hn_lo = _split_bf16(_rms(h, g_ref[...]))
    rw_hi = rw_hi_ref[...]
    logits = _dot(hn_hi, rw_hi) + (_dot(hn_lo, rw_hi) + _dot(hn_hi, rw_lo_ref[...]))
    logits = logits + rb_ref[...]
    lane = lax.broadcasted_iota(jnp.int32, logits.shape, 1)
    remaining = logits
    tops, picks = [], []
    for _ in range(TOP_K):
        top = jnp.max(remaining, axis=1, keepdims=True)
        pick = jnp.min(jnp.where(remaining == top, lane, LANES), axis=1, keepdims=True)
        tops.append(top)
        picks.append(pick)
        remaining = jnp.where(lane == pick, -jnp.inf, remaining)
    exps = [jnp.exp(top - tops[0]) for top in tops]
    denom = sum(exps)
    route = jnp.zeros_like(logits)
    for j, (pick, ex) in enumerate(zip(picks, exps)):
        route = jnp.where(lane == j, pick.astype(F32), route)
        route = jnp.where(lane == TOP_K + j, ex / denom, route)
    route_ref[...] = route
    route_t_ref[...] = route.T[:route_t_ref.shape[0]]


def _out_proj_router(h, o, w_out, g, router_w_pad, router_b_pad):
    rw_hi, rw_lo = _split_bf16(router_w_pad)
    t, d = h.shape
    tm = _pick_tile(t, 512, SUBLANES)
    row = lambda n: pl.BlockSpec((tm, n), lambda i: (i, 0))
    return pl.pallas_call(
        _out_router_kernel,
        grid=(t // tm,),
        in_specs=[row(d), row(o.shape[1]), _resident(w_out.shape), _resident((1, d)),
                  _resident(router_w_pad.shape), _resident(router_w_pad.shape), _resident((1, LANES))],
        out_specs=[row(d), row(LANES), pl.BlockSpec((RECORD_ROWS, tm), lambda i: (0, i))],
        out_shape=[jax.ShapeDtypeStruct((t, d), F32), jax.ShapeDtypeStruct((t, LANES), F32),
                   jax.ShapeDtypeStruct((RECORD_ROWS, t), F32)],
        compiler_params=_params(("parallel",)),
        name="out_proj_router",
    )(h, o, w_out, g, rw_hi, rw_lo, router_b_pad)


def _routing_tables(route_t, n_experts, tile_rows):
    t = route_t.shape[1]
    assert (TOP_K * t) % tile_rows == 0
    idx = route_t[:TOP_K].astype(jnp.int32)
    experts = jnp.arange(n_experts, dtype=jnp.int32)[:, None]
    hits = [idx[j][None, :] == experts for j in range(TOP_K)]
    member = functools.reduce(jnp.logical_or, hits).astype(jnp.int32)
    counts = member.sum(axis=1)
    rank = jnp.cumsum(member, axis=1) - member
    padded = (counts + tile_rows - 1) // tile_rows * tile_rows
    ends = jnp.cumsum(padded)
    starts = ends - padded
    row_of = starts[:, None] + rank
    slots = [jnp.sum(jnp.where(hit, row_of, 0), axis=0).astype(jnp.int32) for hit in hits]
    n_tiles = TOP_K * t // tile_rows + n_experts
    n_used = (ends[-1] // tile_rows).astype(jnp.int32)
    tile_start = jnp.arange(n_tiles, dtype=jnp.int32) * tile_rows
    tile_expert = jnp.sum(tile_start[:, None] >= ends[None, :], axis=1).astype(jnp.int32)
    tile_expert = jnp.where(jnp.arange(n_tiles) < n_used, tile_expert, tile_expert[n_used - 1])
    last_tile = jnp.where(padded > 0, ends // tile_rows - 1, -1)
    trailing = n_used + jnp.arange(n_experts, dtype=jnp.int32)
    trailing = jnp.where(trailing < n_tiles, trailing, -1)
    zero_tiles = jnp.concatenate([last_tile, trailing]).astype(jnp.int32)
    return slots, tile_expert, n_used.reshape(1), zero_tiles


def _row_copy(src_ref, src_row, dst_ref, dst_row, sem):
    return pltpu.make_async_copy(src_ref.at[pl.ds(src_row, 1)], dst_ref.at[pl.ds(dst_row, 1)], sem)


def _dispatch_kernel(zero_tiles_ref, *refs):
    slot_refs, (h_ref, xs_hbm, zeros_ref, sem) = refs[:TOP_K], refs[TOP_K:]
    tm = h_ref.shape[0]
    tile_rows = zeros_ref.shape[0]

    @pl.when(pl.program_id(0) == 0)
    def _():
        zeros_ref[...] = jnp.zeros_like(zeros_ref)

        def fill(j):
            row = pl.multiple_of(jnp.maximum(zero_tiles_ref[j], 0) * tile_rows, tile_rows)
            return pltpu.make_async_copy(zeros_ref, xs_hbm.at[pl.ds(row, tile_rows)], sem)

        for j in range(zero_tiles_ref.shape[0]):
            pl.when(zero_tiles_ref[j] >= 0)(fill(j).start)
        for j in range(zero_tiles_ref.shape[0]):
            pl.when(zero_tiles_ref[j] >= 0)(fill(j).wait)

    @pl.loop(0, tm, unroll=ROW_DMA_UNROLL)
    def _(i):
        for j in range(TOP_K):
            _row_copy(h_ref, i, xs_hbm, slot_refs[j][i], sem).start()

    for _ in range(TOP_K):
        pltpu.make_async_copy(h_ref, xs_hbm.at[pl.ds(0, tm)], sem).wait()


def _dispatch(h, slots, zero_tiles, n_rows, tile_rows):
    t, d = h.shape
    tm = _pick_tile(t, 1024, SUBLANES)
    return pl.pallas_call(
        _dispatch_kernel,
        grid=(t // tm,),
        in_specs=[pl.BlockSpec(zero_tiles.shape, lambda i: (0,), memory_space=pltpu.SMEM)]
        + [pl.BlockSpec((tm,), lambda i: (i,), memory_space=pltpu.SMEM)] * TOP_K
        + [pl.BlockSpec((tm, d), lambda i: (i, 0))],
        out_specs=pl.BlockSpec(memory_space=pl.ANY),
        out_shape=jax.ShapeDtypeStruct((n_rows, d), F32),
        scratch_shapes=[pltpu.VMEM((tile_rows, d), F32), pltpu.SemaphoreType.DMA(())],
        compiler_params=_params(("arbitrary",)),
        name="moe_dispatch",
    )(zero_tiles, *slots, h)


def _grouped_ffn_kernel(te_ref, nu_ref, x_ref, g_ref, wg_ref, wu_ref, wd_ref, y_ref):
    del te_ref
    used = pl.program_id(0) < nu_ref[0]

    @pl.when(jnp.logical_not(used))
    def _():
        y_ref[...] = jnp.zeros_like(y_ref)

    @pl.when(used)
    def _():
        xn = _rms(x_ref[...], g_ref[...]).astype(BF16)
        act = _silu(_dot(xn, wg_ref[0])) * _dot(xn, wu_ref[0])
        y_ref[...] = _dot(act.astype(BF16), wd_ref[0])


def _grouped_ffn(xs, g, w_gate, w_up, w_down, tile_expert, n_used, *, tile_rows):
    n_rows, d = xs.shape
    ff = w_gate.shape[2]

    def x_map(i, te, nu):
        return (jnp.minimum(i, nu[0] - 1), 0)

    def expert_block(shape):
        return pl.BlockSpec((1,) + shape, lambda i, te, nu: (te[i], 0, 0), pipeline_mode=pl.Buffered(1))

    return pl.pallas_call(
        _grouped_ffn_kernel,
        grid_spec=pltpu.PrefetchScalarGridSpec(
            num_scalar_prefetch=2,
            grid=(n_rows // tile_rows,),
            in_specs=[
                pl.BlockSpec((tile_rows, d), x_map),
                pl.BlockSpec((1, d), lambda i, te, nu: (0, 0)),
                expert_block((d, ff)),
                expert_block((d, ff)),
                expert_block((ff, d)),
            ],
            out_specs=pl.BlockSpec((tile_rows, d), lambda i, te, nu: (i, 0)),
        ),
        out_shape=jax.ShapeDtypeStruct((n_rows, d), F32),
        compiler_params=_params(("arbitrary",)),
        name="moe_grouped_ffn",
    )(tile_expert, n_used, xs, g, w_gate, w_up, w_down)


def _combine_kernel(*refs):
    slot_refs, (h_ref, route_ref, gf_ref, y_hbm, o_ref, ybuf_ref, sem) = refs[:TOP_K], refs[TOP_K:]
    tm = h_ref.shape[0]

    @pl.loop(0, tm, unroll=ROW_DMA_UNROLL)
    def _(i):
        for j in range(TOP_K):
            _row_copy(y_hbm, slot_refs[j][i], ybuf_ref.at[j], i, sem).start()

    for j in range(TOP_K):
        pltpu.make_async_copy(y_hbm.at[pl.ds(0, tm)], ybuf_ref.at[j], sem).wait()

    route = route_ref[...]
    out = h_ref[...]
    for j in range(TOP_K):
        out = out + route[:, TOP_K + j:TOP_K + j + 1] * ybuf_ref[j]
    o_ref[...] = _rms(out, gf_ref[...])


def _combine(h, route, slots, y, final_g):
    t, d = h.shape
    tm = _pick_tile(t, 512, SUBLANES)
    row = lambda n: pl.BlockSpec((tm, n), lambda i: (i, 0))
    return pl.pallas_call(
        _combine_kernel,
        grid=(t // tm,),
        in_specs=[pl.BlockSpec((tm,), lambda i: (i,), memory_space=pltpu.SMEM)] * TOP_K
        + [row(d), row(LANES), pl.BlockSpec((1, d), lambda i: (0, 0)), pl.BlockSpec(memory_space=pl.ANY)],
        out_specs=row(d),
        out_shape=jax.ShapeDtypeStruct((t, d), F32),
        scratch_shapes=[pltpu.VMEM((TOP_K, tm, d), F32), pltpu.SemaphoreType.DMA(())],
        compiler_params=_params(("arbitrary",)),
        name="moe_combine",
    )(*slots, h, route, final_g, y)


def _pad_lanes(a, offset=0, fill=0.0):
    n = a.shape[-1]
    pad = [(0, 0)] * (a.ndim - 1) + [(offset, LANES - offset - n)]
    return jnp.pad(a, pad, constant_values=fill)


def kernel(x, norm_mix_g, norm_ffn_g, pool_w_in, pool_w_group, pool_scale, dn_w_in, dn_conv_w, dn_a_log, dn_dt_bias, dn_norm_g, dn_w_out, ffn_w_gate, ffn_w_up, ffn_w_down, moe_router_w, moe_router_b, moe_w_gate, moe_w_up, moe_w_down, final_norm_g):
    b, s, d = x.shape
    t = b * s
    kd = dn_w_out.shape[1]
    heads = kd // DN_HEAD_DIM
    n_experts = moe_router_w.shape[-1]
    assert 2 * heads <= RECORD_ROWS and n_experts <= LANES and 2 * TOP_K <= RECORD_ROWS
    vec = lambda a: a.reshape(1, -1).astype(F32)

    h = x.reshape(t, d)

    h = _layer0(h, s, vec(norm_mix_g[0]), pool_w_in[0].astype(BF16), pool_w_group[0].astype(BF16),
                vec(pool_scale[0]), vec(norm_ffn_g[0]), ffn_w_gate[0].astype(BF16), ffn_w_up[0].astype(BF16),
                ffn_w_down[0].astype(BF16))

    w_in = dn_w_in[0]
    w_small = _pad_lanes(w_in[:, 4 * kd:]).astype(BF16)
    q, k, v, z, bg, bg_t = _dn_in_proj(
        h, s, vec(norm_mix_g[1]), w_in[:, :3 * kd].astype(BF16), w_in[:, 3 * kd:4 * kd].astype(BF16), w_small,
        dn_conv_w[0].astype(F32), _pad_lanes(vec(dn_a_log[0]), heads), _pad_lanes(vec(dn_dt_bias[0]), heads))
    g_rows = bg_t[heads:2 * heads].reshape(heads, b, s // DN_CHUNK, DN_CHUNK).transpose(1, 2, 0, 3)
    shape3 = lambda a: a.reshape(b, s, a.shape[-1])
    o = _delta_rule(shape3(q), shape3(k), shape3(v), shape3(z), shape3(bg), g_rows, vec(dn_norm_g[0]),
                    seqs_per_step=4 if b % 4 == 0 else (2 if b % 2 == 0 else 1))
    h, route, route_t = _out_proj_router(
        h, o.reshape(t, kd), dn_w_out[0].astype(BF16), vec(norm_ffn_g[1]),
        _pad_lanes(moe_router_w[0].astype(F32)), _pad_lanes(vec(moe_router_b[0]), fill=-jnp.inf))

    tile_rows = _pick_tile(TOP_K * t, MOE_TILE_ROWS, SUBLANES)
    slots, tile_expert, n_used, zero_tiles = _routing_tables(route_t, n_experts, tile_rows)
    xs = _dispatch(h, slots, zero_tiles, tile_expert.shape[0] * tile_rows, tile_rows)
    y = _grouped_ffn(xs, vec(norm_ffn_g[1]), moe_w_gate[0].astype(BF16), moe_w_up[0].astype(BF16),
                     moe_w_down[0].astype(BF16), tile_expert, n_used, tile_rows=tile_rows)
    out = _combine(h, route, slots, y, vec(final_norm_g))
    return out.reshape(b, s, d)
```

```python
import functools

import jax
import jax.numpy as jnp
from jax import lax
from jax.experimental import pallas as pl
from jax.experimental.pallas import tpu as pltpu

F32 = jnp.float32
BF16 = jnp.bfloat16

RMS_EPS = 1e-6
POOL_WINDOWS = (2, 4, 8, 16)
DN_HEAD_DIM = 128
DN_CHUNK = 64
TOP_K = 2

LANES = 128
SUBLANES = 8
VMEM_LIMIT_BYTES = 56 * 1024 * 1024

POOL_HALO = 16
CONV_HALO = SUBLANES

MOE_TILE_ROWS = 512
ROW_GROUP = SUBLANES
WEIGHT_CHUNKS = 8
RECORD_ROWS = 16


def _dot(a, b):
    return jnp.dot(a, b, preferred_element_type=F32)


def _dot_f32(a, b):
    return jnp.dot(a, b, preferred_element_type=F32, precision=lax.Precision.HIGHEST)


def _dot_nt(a, b):
    return lax.dot_general(a, b, (((1,), (1,)), ((), ())), preferred_element_type=F32)


def _dot_tn(a, b):
    return lax.dot_general(a, b, (((0,), (0,)), ((), ())), preferred_element_type=F32)


def _rms(x, g):
    return x * lax.rsqrt(jnp.mean(x * x, axis=-1, keepdims=True) + RMS_EPS) * g


def _silu(x):
    return x * jax.nn.sigmoid(x)


def _pick_tile(n, target, quantum):
    if n <= target:
        return n
    best = None
    for t in range(quantum, target + 1, quantum):
        if n % t == 0:
            best = t
    assert best is not None, (n, target, quantum)
    return best


def _params(semantics):
    return pltpu.CompilerParams(dimension_semantics=semantics, vmem_limit_bytes=VMEM_LIMIT_BYTES)


def _resident(shape):
    nd = len(shape)
    return pl.BlockSpec(shape, lambda *_: (0,) * nd, pipeline_mode=pl.Buffered(1))


def _layer0_kernel(x_ref, xh_ref, gm_ref, win_ref, wgrp_ref, sc_ref, gf_ref, wg_ref, wu_ref, wd_ref,
                   o_ref, ext_ref, h_ref, *, tiles_per_seq):
    tm = x_ref.shape[0]
    j = pl.program_id(0) % tiles_per_seq
    x = x_ref[...]
    gm = gm_ref[...]
    win = win_ref[...]
    u = _dot(_rms(x, gm).astype(BF16), win)
    uh = _dot(_rms(xh_ref[...], gm).astype(BF16), win)
    ext_ref[0:POOL_HALO, :] = jnp.where(j == 0, 0.0, uh)
    ext_ref[POOL_HALO:, :] = u
    pos = (j * tm + lax.broadcasted_iota(jnp.int32, (tm, 1), 0) + 1).astype(F32)
    c = u.shape[1] // len(POOL_WINDOWS)
    for gi, w in enumerate(POOL_WINDOWS):
        sl = slice(gi * c, (gi + 1) * c)
        s = ext_ref[:, sl]
        sh = 1
        while sh < w:
            s = s + pltpu.roll(s, sh, 0)
            sh *= 2
        m = s[POOL_HALO:] / jnp.minimum(pos, float(w)) - u[:, sl]
        y = _dot(m.astype(BF16), wgrp_ref[gi])
        h_ref[:, sl] = x[:, sl] + y * sc_ref[:, sl]
    h = h_ref[...]
    hn = _rms(h, gf_ref[...]).astype(BF16)
    act = _silu(_dot(hn, wg_ref[...])) * _dot(hn, wu_ref[...])
    o_ref[...] = h + _dot(act.astype(BF16), wd_ref[...])


def _layer0(x2d, seq, g_mix, w_in, w_group, scale, g_ffn, w_gate, w_up, w_down):
    t, d = x2d.shape
    assert all(w & (w - 1) == 0 and w <= POOL_HALO for w in POOL_WINDOWS)
    tm = _pick_tile(seq, 512, POOL_HALO)
    assert tm % POOL_HALO == 0
    halo_blocks = tm // POOL_HALO
    row = pl.BlockSpec((tm, d), lambda i: (i, 0))
    return pl.pallas_call(
        functools.partial(_layer0_kernel, tiles_per_seq=seq // tm),
        grid=(t // tm,),
        in_specs=[
            row,
            pl.BlockSpec((POOL_HALO, d), lambda i: (jnp.maximum(i * halo_blocks - 1, 0), 0)),
            _resident((1, d)), _resident(w_in.shape), _resident(w_group.shape), _resident((1, d)),
            _resident((1, d)), _resident(w_gate.shape), _resident(w_up.shape), _resident(w_down.shape),
        ],
        out_specs=row,
        out_shape=jax.ShapeDtypeStruct((t, d), F32),
        scratch_shapes=[pltpu.VMEM((tm + POOL_HALO, d), F32), pltpu.VMEM((tm, d), F32)],
        compiler_params=_params(("parallel",)),
        name="layer0_pool_ffn",
    )(x2d, x2d, g_mix, w_in, w_group, scale, g_ffn, w_gate, w_up, w_down)


def _dn_in_kernel(h_ref, hh_ref, g_ref, wqkv_ref, wz_ref, wsm_ref, cw_ref, alog_ref, dtb_ref,
                  q_ref, k_ref, v_ref, z_ref, bg_ref, bgt_ref, ext_ref, *, tiles_per_seq):
    j = pl.program_id(0) % tiles_per_seq
    g = g_ref[...]
    hn = _rms(h_ref[...], g).astype(BF16)
    hnh = jnp.where(j == 0, 0.0, _rms(hh_ref[...], g)).astype(BF16)
    kd = q_ref.shape[1]
    heads = kd // DN_HEAD_DIM
    taps = cw_ref.shape[0]
    for part, out_ref in enumerate((q_ref, k_ref, v_ref)):
        cols = slice(part * kd, (part + 1) * kd)
        w = wqkv_ref[:, cols]
        ext_ref[0:CONV_HALO, :] = _dot(hnh, w)
        ext_ref[CONV_HALO:, :] = _dot(hn, w)
        p = ext_ref[...]
        cw = cw_ref[:, cols]
        y = p * cw[taps - 1:taps, :]
        for s in range(1, taps):
            y = y + pltpu.roll(p, s, 0) * cw[taps - 1 - s:taps - s, :]
        y = _silu(y[CONV_HALO:])
        if part == 2:
            out_ref[...] = y
            continue
        for h in range(heads):
            hs = slice(h * DN_HEAD_DIM, (h + 1) * DN_HEAD_DIM)
            yh = y[:, hs]
            yh = yh * lax.rsqrt(jnp.sum(yh * yh, axis=-1, keepdims=True) + RMS_EPS)
            if part == 0:
                yh = yh * (DN_HEAD_DIM ** -0.5)
            out_ref[:, hs] = yh
    z_ref[...] = _dot(hn, wz_ref[...])
    sm = _dot(hn, wsm_ref[...])
    xs = sm + dtb_ref[...]
    softplus = jnp.maximum(xs, 0.0) + jnp.log1p(jnp.exp(-jnp.abs(xs)))
    lane = lax.broadcasted_iota(jnp.int32, sm.shape, 1)
    bg = jnp.where(lane < heads, jax.nn.sigmoid(sm), -jnp.exp(alog_ref[...]) * softplus)
    bg_ref[...] = bg
    bgt_ref[...] = bg.T[:bgt_ref.shape[0]]


def _dn_in_proj(h, seq, g, w_qkv, w_z, w_small, conv_w, a_log_pad, dt_bias_pad):
    t, d = h.shape
    kd = w_z.shape[1]
    assert conv_w.shape[0] - 1 <= CONV_HALO
    tm = _pick_tile(seq, 512, CONV_HALO)
    halo_blocks = tm // CONV_HALO
    row = lambda n: pl.BlockSpec((tm, n), lambda i: (i, 0))
    return pl.pallas_call(
        functools.partial(_dn_in_kernel, tiles_per_seq=seq // tm),
        grid=(t // tm,),
        in_specs=[
            row(d),
            pl.BlockSpec((CONV_HALO, d), lambda i: (jnp.maximum(i * halo_blocks - 1, 0), 0)),
            _resident((1, d)),
            _resident(w_qkv.shape),
            _resident(w_z.shape),
            _resident(w_small.shape),
            _resident(conv_w.shape),
            _resident((1, LANES)),
            _resident((1, LANES)),
        ],
        out_specs=[row(kd), row(kd), row(kd), row(kd), row(LANES),
                   pl.BlockSpec((RECORD_ROWS, tm), lambda i: (0, i))],
        out_shape=[jax.ShapeDtypeStruct((t, kd), F32)] * 4 + [jax.ShapeDtypeStruct((t, LANES), F32),
                                                              jax.ShapeDtypeStruct((RECORD_ROWS, t), F32)],
        scratch_shapes=[pltpu.VMEM((tm + CONV_HALO, kd), F32)],
        compiler_params=_params(("parallel",)),
        name="deltanet_in_proj",
    )(h, h, g, w_qkv, w_z, w_small, conv_w, a_log_pad, dt_bias_pad)


def _delta_kernel(q_ref, k_ref, v_ref, z_ref, bg_ref, gt_ref, ng_ref, o_ref, state_ref):
    nb, c, kd = q_ref.shape
    heads = kd // DN_HEAD_DIM
    pairs = [(b, h) for b in range(nb) for h in range(heads)]

    @pl.when(pl.program_id(1) == 0)
    def _():
        state_ref[...] = jnp.zeros_like(state_ref)

    def head_cols(h):
        return slice(h * DN_HEAD_DIM, (h + 1) * DN_HEAD_DIM)

    row = lax.broadcasted_iota(jnp.int32, (c, c), 0)
    col = lax.broadcasted_iota(jnp.int32, (c, c), 1)
    causal = row >= col
    strict = row > col
    diag = row == col
    lower_ones = causal.astype(F32)
    upper_ones = (row <= col).astype(F32)
    bg = [bg_ref[b] for b in range(nb)]
    g_cum_col = [_dot_f32(lower_ones, bg[b]) for b in range(nb)]
    g_cum_row = [_dot_f32(gt_ref[b, 0], upper_ones) for b in range(nb)]

    q = [q_ref[b, :, head_cols(h)] for b, h in pairs]
    k = [k_ref[b, :, head_cols(h)] for b, h in pairs]
    beta = [bg[b][:, h:h + 1] for b, h in pairs]
    gc = [g_cum_col[b][:, heads + h:heads + h + 1] for b, h in pairs]
    gr = [g_cum_row[b][h:h + 1, :] for b, h in pairs]
    n = len(pairs)
    rng = range(n)
    decay = [jnp.where(causal, jnp.exp(jnp.where(causal, gc[i] - gr[i], 0.0)), 0.0) for i in rng]
    k_beta = [k[i] * beta[i] for i in rng]
    scores = [_dot_nt(jnp.concatenate([k_beta[i], q[i]], axis=0).astype(BF16), k[i].astype(BF16))
              for i in rng]
    qk = [(scores[i][c:] * decay[i]).astype(BF16) for i in rng]
    x = [jnp.where(strict, -(scores[i][:c] * decay[i]), 0.0) for i in rng]
    inv = [jnp.where(diag, 1.0, x[i]) for i in rng]
    p = 1
    while 2 * p < c:
        x16 = [x[i].astype(BF16) for i in rng]
        x = [_dot(x16[i], x16[i]) for i in rng]
        inv = [inv[i] + _dot(inv[i].astype(BF16), x[i].astype(BF16)) for i in rng]
        p *= 2
    e_gc = [jnp.exp(gc[i]) for i in rng]
    rhs = [jnp.concatenate([v_ref[b, :, head_cols(h)] * beta[i], k_beta[i] * e_gc[i]], axis=1).astype(BF16)
           for i, (b, h) in enumerate(pairs)]
    uw = [_dot(inv[i].astype(BF16), rhs[i]) for i in rng]
    state = [state_ref[i] for i in rng]
    ws_qs = [_dot(jnp.concatenate([uw[i][:, DN_HEAD_DIM:], q[i] * e_gc[i]], axis=0).astype(BF16),
                  state[i].astype(BF16)) for i in rng]
    v_new = [(uw[i][:, :DN_HEAD_DIM] - ws_qs[i][:c]).astype(BF16) for i in rng]
    o = [ws_qs[i][c:] + _dot(qk[i], v_new[i]) for i in rng]
    g_last = [gc[i][c - 1:c, :] for i in rng]
    k_dec = [(k[i] * jnp.exp(g_last[i] - gc[i])).astype(BF16) for i in rng]
    for i in rng:
        state_ref[i] = state[i] * jnp.exp(g_last[i]) + _dot_tn(k_dec[i], v_new[i])
    ng = ng_ref[...]
    for i, (b, h) in enumerate(pairs):
        on = o[i] * lax.rsqrt(jnp.mean(o[i] * o[i], axis=-1, keepdims=True) + RMS_EPS) * ng
        o_ref[b, :, head_cols(h)] = (on * _silu(z_ref[b, :, head_cols(h)])).astype(BF16)


def _delta_rule(q, k, v, z, bg, g_rows, norm_g, *, seqs_per_step):
    b, s, kd = q.shape
    heads = kd // DN_HEAD_DIM
    c = DN_CHUNK
    nb = seqs_per_step
    assert s % c == 0 and c & (c - 1) == 0 and b % nb == 0
    blk = lambda n: pl.BlockSpec((nb, c, n), lambda i, j: (i, j, 0))
    return pl.pallas_call(
        _delta_kernel,
        grid=(b // nb, s // c),
        in_specs=[blk(kd), blk(kd), blk(kd), blk(kd), blk(LANES),
                  pl.BlockSpec((nb, 1, heads, c), lambda i, j: (i, j, 0, 0)),
                  pl.BlockSpec((1, DN_HEAD_DIM), lambda i, j: (0, 0))],
        out_specs=blk(kd),
        out_shape=jax.ShapeDtypeStruct((b, s, kd), BF16),
        scratch_shapes=[pltpu.VMEM((nb * heads, DN_HEAD_DIM, DN_HEAD_DIM), F32)],
        compiler_params=_params(("parallel", "arbitrary")),
        name="delta_rule",
    )(q, k, v, z, bg, g_rows, norm_g)


def _split_bf16(a):
    hi = a.astype(BF16)
    return hi, (a - hi.astype(F32)).astype(BF16)


def _out_router_kernel(h_ref, o_ref, wo_ref, g_ref, rw_hi_ref, rw_lo_ref, rb_ref, h_out_ref, route_ref, route_t_ref):
    h = h_ref[...] + _dot(o_ref[...], wo_ref[...])
    h_out_ref[...] = h
    hn_hi, hn_lo = _split_bf16(_rms(h, g_ref[...]))
    rw_hi = rw_hi_ref[...]
    logits = _dot(hn_hi, rw_hi) + (_dot(hn_lo, rw_hi) + _dot(hn_hi, rw_lo_ref[...]))
    logits = logits + rb_ref[...]
    lane = lax.broadcasted_iota(jnp.int32, logits.shape, 1)
    remaining = logits
    tops, picks = [], []
    for _ in range(TOP_K):
        top = jnp.max(remaining, axis=1, keepdims=True)
        pick = jnp.min(jnp.where(remaining == top, lane, LANES), axis=1, keepdims=True)
        tops.append(top)
        picks.append(pick)
        remaining = jnp.where(lane == pick, -jnp.inf, remaining)
    exps = [jnp.exp(top - tops[0]) for top in tops]
    denom = sum(exps)
    route = jnp.zeros_like(logits)
    for j, (pick, ex) in enumerate(zip(picks, exps)):
        route = jnp.where(lane == j, pick.astype(F32), route)
        route = jnp.where(lane == TOP_K + j, ex / denom, route)
    route_ref[...] = route
    route_t_ref[...] = route.T[:route_t_ref.shape[0]]


def _out_proj_router(h, o, w_out, g, router_w_pad, router_b_pad):
    rw_hi, rw_lo = _split_bf16(router_w_pad)
    t, d = h.shape
    tm = _pick_tile(t, 512, SUBLANES)
    row = lambda n: pl.BlockSpec((tm, n), lambda i: (i, 0))
    return pl.pallas_call(
        _out_router_kernel,
        grid=(t // tm,),
        in_specs=[row(d), row(o.shape[1]), _resident(w_out.shape), _resident((1, d)),
                  _resident(router_w_pad.shape), _resident(router_w_pad.shape), _resident((1, LANES))],
        out_specs=[row(d), row(LANES), pl.BlockSpec((RECORD_ROWS, tm), lambda i: (0, i))],
        out_shape=[jax.ShapeDtypeStruct((t, d), F32), jax.ShapeDtypeStruct((t, LANES), F32),
                   jax.ShapeDtypeStruct((RECORD_ROWS, t), F32)],
        compiler_params=_params(("parallel",)),
        name="out_proj_router",
    )(h, o, w_out, g, rw_hi, rw_lo, router_b_pad)


def _routing_tables(route_t, n_experts, tile_rows):
    t = route_t.shape[1]
    assert (TOP_K * t) % tile_rows == 0
    idx = route_t[:TOP_K].astype(jnp.int32)
    experts = jnp.arange(n_experts, dtype=jnp.int32)[:, None]
    hits = [idx[j][None, :] == experts for j in range(TOP_K)]
    member = functools.reduce(jnp.logical_or, hits).astype(jnp.int32)
    counts = member.sum(axis=1)
    rank = jnp.cumsum(member, axis=1) - member
    padded = (counts + tile_rows - 1) // tile_rows * tile_rows
    ends = jnp.cumsum(padded)
    starts = ends - padded
    row_of = starts[:, None] + rank
    slots = [jnp.sum(jnp.where(hit, row_of, 0), axis=0).astype(jnp.int32) for hit in hits]
    n_tiles = TOP_K * t // tile_rows + n_experts
    n_used = (ends[-1] // tile_rows).astype(jnp.int32)
    tile_start = jnp.arange(n_tiles, dtype=jnp.int32) * tile_rows
    tile_expert = jnp.sum(tile_start[:, None] >= ends[None, :], axis=1).astype(jnp.int32)
    tile_expert = jnp.where(jnp.arange(n_tiles) < n_used, tile_expert, tile_expert[n_used - 1])
    last_tile = jnp.where(padded > 0, ends // tile_rows - 1, -1)
    trailing = n_used + jnp.arange(n_experts, dtype=jnp.int32)
    trailing = jnp.where(trailing < n_tiles, trailing, -1)
    zero_tiles = jnp.concatenate([last_tile, trailing]).astype(jnp.int32)
    return slots, tile_expert, n_used.reshape(1), zero_tiles


def _row_copy(src_ref, src_row, dst_ref, dst_row, sem):
    return pltpu.make_async_copy(src_ref.at[pl.ds(src_row, 1)], dst_ref.at[pl.ds(dst_row, 1)], sem)


def _dispatch_kernel(zero_tiles_ref, *refs):
    slot_refs, (h_ref, xs_hbm, zeros_ref, sem) = refs[:TOP_K], refs[TOP_K:]
    tm = h_ref.shape[0]
    tile_rows = zeros_ref.shape[0]

    @pl.when(pl.program_id(0) == 0)
    def _():
        zeros_ref[...] = jnp.zeros_like(zeros_ref)

        def fill(j):
            row = pl.multiple_of(jnp.maximum(zero_tiles_ref[j], 0) * tile_rows, tile_rows)
            return pltpu.make_async_copy(zeros_ref, xs_hbm.at[pl.ds(row, tile_rows)], sem)

        for j in range(zero_tiles_ref.shape[0]):
            pl.when(zero_tiles_ref[j] >= 0)(fill(j).start)
        for j in range(zero_tiles_ref.shape[0]):
            pl.when(zero_tiles_ref[j] >= 0)(fill(j).wait)

    @pl.loop(0, tm // ROW_GROUP)
    def _(g):
        base = pl.multiple_of(g * ROW_GROUP, ROW_GROUP)
        rows = h_ref.at[pl.ds(base, ROW_GROUP)]
        for u in range(ROW_GROUP):
            for j in range(TOP_K):
                _row_copy(rows, u, xs_hbm, slot_refs[j][base + u], sem).start(priority=j % 2)

    for _ in range(TOP_K):
        pltpu.make_async_copy(h_ref, xs_hbm.at[pl.ds(0, tm)], sem).wait()


def _dispatch(h, slots, zero_tiles, n_rows, tile_rows):
    t, d = h.shape
    tm = _pick_tile(t, 1024, SUBLANES)
    return pl.pallas_call(
        _dispatch_kernel,
        grid=(t // tm,),
        in_specs=[pl.BlockSpec(zero_tiles.shape, lambda i: (0,), memory_space=pltpu.SMEM)]
        + [pl.BlockSpec((tm,), lambda i: (i,), memory_space=pltpu.SMEM)] * TOP_K
        + [pl.BlockSpec((tm, d), lambda i: (i, 0))],
        out_specs=pl.BlockSpec(memory_space=pl.ANY),
        out_shape=jax.ShapeDtypeStruct((n_rows, d), F32),
        scratch_shapes=[pltpu.VMEM((tile_rows, d), F32), pltpu.SemaphoreType.DMA(())],
        compiler_params=_params(("arbitrary",)),
        name="moe_dispatch",
    )(zero_tiles, *slots, h)


def _grouped_ffn_kernel(te_ref, nu_ref, x_ref, g_ref, wg_hbm, wu_hbm, wd_hbm, y_ref,
                        wg_ref, wu_ref, wd_ref, stage_in_ref, stage_out_ref, sems):
    i = pl.program_id(0)
    used = i < nu_ref[0]
    expert = te_ref[i]

    @pl.when(jnp.logical_not(used))
    def _():
        y_ref[...] = jnp.zeros_like(y_ref)

    @pl.when(used & ((i == 0) | (te_ref[jnp.maximum(i - 1, 0)] != expert)))
    def _():
        chunks = []
        for src, stage, dst in ((wg_hbm, stage_in_ref, wg_ref), (wu_hbm, stage_in_ref, wu_ref),
                                (wd_hbm, stage_out_ref, wd_ref)):
            rows = stage.shape[1]
            for c in range(dst.shape[0] // rows):
                chunks.append((src.at[expert, pl.ds(c * rows, rows)], stage, dst.at[pl.ds(c * rows, rows)]))

        def copy(n):
            src, stage, _ = chunks[n]
            return pltpu.make_async_copy(src, stage.at[n % 2], sems.at[n % 2])

        copy(0).start()
        for n in range(len(chunks)):
            copy(n).wait()
            if n + 1 < len(chunks):
                copy(n + 1).start()
            _, stage, dst = chunks[n]
            dst[...] = stage[n % 2].astype(BF16)

    @pl.when(used)
    def _():
        xn = _rms(x_ref[...], g_ref[...]).astype(BF16)
        act = _silu(_dot(xn, wg_ref[...])) * _dot(xn, wu_ref[...])
        y_ref[...] = _dot(act.astype(BF16), wd_ref[...])


def _grouped_ffn(xs, g, w_gate, w_up, w_down, tile_expert, n_used, *, tile_rows):
    n_rows, d = xs.shape
    ff = w_gate.shape[2]
    assert d % WEIGHT_CHUNKS == 0 and ff % WEIGHT_CHUNKS == 0

    def x_map(i, te, nu):
        return (jnp.minimum(i, nu[0] - 1), 0)

    return pl.pallas_call(
        _grouped_ffn_kernel,
        grid_spec=pltpu.PrefetchScalarGridSpec(
            num_scalar_prefetch=2,
            grid=(n_rows // tile_rows,),
            in_specs=[
                pl.BlockSpec((tile_rows, d), x_map),
                pl.BlockSpec((1, d), lambda i, te, nu: (0, 0)),
                pl.BlockSpec(memory_space=pl.ANY),
                pl.BlockSpec(memory_space=pl.ANY),
                pl.BlockSpec(memory_space=pl.ANY),
            ],
            out_specs=pl.BlockSpec((tile_rows, d), lambda i, te, nu: (i, 0)),
            scratch_shapes=[
                pltpu.VMEM((d, ff), BF16), pltpu.VMEM((d, ff), BF16), pltpu.VMEM((ff, d), BF16),
                pltpu.VMEM((2, d // WEIGHT_CHUNKS, ff), F32), pltpu.VMEM((2, ff // WEIGHT_CHUNKS, d), F32),
                pltpu.SemaphoreType.DMA((2,)),
            ],
        ),
        out_shape=jax.ShapeDtypeStruct((n_rows, d), F32),
        compiler_params=_params(("arbitrary",)),
        name="moe_grouped_ffn",
    )(tile_expert, n_used, xs, g, w_gate, w_up, w_down)


def _combine_kernel(*refs):
    slot_refs, (h_ref, route_ref, gf_ref, y_hbm, o_ref, ybuf_ref, sem) = refs[:TOP_K], refs[TOP_K:]
    tm = h_ref.shape[0]

    @pl.loop(0, tm // ROW_GROUP)
    def _(g):
        base = pl.multiple_of(g * ROW_GROUP, ROW_GROUP)
        for j in range(TOP_K):
            rows = ybuf_ref.at[j, pl.ds(base, ROW_GROUP)]
            for u in range(ROW_GROUP):
                _row_copy(y_hbm, slot_refs[j][base + u], rows, u, sem).start(priority=j % 2)

    for j in range(TOP_K):
        pltpu.make_async_copy(y_hbm.at[pl.ds(0, tm)], ybuf_ref.at[j], sem).wait()

    route = route_ref[...]
    out = h_ref[...]
    for j in range(TOP_K):
        out = out + route[:, TOP_K + j:TOP_K + j + 1] * ybuf_ref[j]
    o_ref[...] = _rms(out, gf_ref[...])


def _combine(h, route, slots, y, final_g):
    t, d = h.shape
    tm = _pick_tile(t, 512, SUBLANES)
    row = lambda n: pl.BlockSpec((tm, n), lambda i: (i, 0))
    return pl.pallas_call(
        _combine_kernel,
        grid=(t // tm,),
        in_specs=[pl.BlockSpec((tm,), lambda i: (i,), memory_space=pltpu.SMEM)] * TOP_K
        + [row(d), row(LANES), pl.BlockSpec((1, d), lambda i: (0, 0)), pl.BlockSpec(memory_space=pl.ANY)],
        out_specs=row(d),
        out_shape=jax.ShapeDtypeStruct((t, d), F32),
        scratch_shapes=[pltpu.VMEM((TOP_K, tm, d), F32), pltpu.SemaphoreType.DMA(())],
        compiler_params=_params(("arbitrary",)),
        name="moe_combine",
    )(*slots, h, route, final_g, y)


def _pad_lanes(a, offset=0, fill=0.0):
    n = a.shape[-1]
    pad = [(0, 0)] * (a.ndim - 1) + [(offset, LANES - offset - n)]
    return jnp.pad(a, pad, constant_values=fill)


def kernel(x, norm_mix_g, norm_ffn_g, pool_w_in, pool_w_group, pool_scale, dn_w_in, dn_conv_w, dn_a_log, dn_dt_bias, dn_norm_g, dn_w_out, ffn_w_gate, ffn_w_up, ffn_w_down, moe_router_w, moe_router_b, moe_w_gate, moe_w_up, moe_w_down, final_norm_g):
    b, s, d = x.shape
    t = b * s
    kd = dn_w_out.shape[1]
    heads = kd // DN_HEAD_DIM
    n_experts = moe_router_w.shape[-1]
    assert 2 * heads <= RECORD_ROWS and n_experts <= LANES and 2 * TOP_K <= RECORD_ROWS
    vec = lambda a: a.reshape(1, -1).astype(F32)

    h = x.reshape(t, d)

    h = _layer0(h, s, vec(norm_mix_g[0]), pool_w_in[0].astype(BF16), pool_w_group[0].astype(BF16),
                vec(pool_scale[0]), vec(norm_ffn_g[0]), ffn_w_gate[0].astype(BF16), ffn_w_up[0].astype(BF16),
                ffn_w_down[0].astype(BF16))

    w_in = dn_w_in[0]
    w_small = _pad_lanes(w_in[:, 4 * kd:]).astype(BF16)
    q, k, v, z, bg, bg_t = _dn_in_proj(
        h, s, vec(norm_mix_g[1]), w_in[:, :3 * kd].astype(BF16), w_in[:, 3 * kd:4 * kd].astype(BF16), w_small,
        dn_conv_w[0].astype(F32), _pad_lanes(vec(dn_a_log[0]), heads), _pad_lanes(vec(dn_dt_bias[0]), heads))
    g_rows = bg_t[heads:2 * heads].reshape(heads, b, s // DN_CHUNK, DN_CHUNK).transpose(1, 2, 0, 3)
    shape3 = lambda a: a.reshape(b, s, a.shape[-1])
    o = _delta_rule(shape3(q), shape3(k), shape3(v), shape3(z), shape3(bg), g_rows, vec(dn_norm_g[0]),
                    seqs_per_step=4 if b % 4 == 0 else (2 if b % 2 == 0 else 1))
    h, route, route_t = _out_proj_router(
        h, o.reshape(t, kd), dn_w_out[0].astype(BF16), vec(norm_ffn_g[1]),
        _pad_lanes(moe_router_w[0].astype(F32)), _pad_lanes(vec(moe_router_b[0]), fill=-jnp.inf))

    tile_rows = _pick_tile(TOP_K * t, MOE_TILE_ROWS, SUBLANES)
    slots, tile_expert, n_used, zero_tiles = _routing_tables(route_t, n_experts, tile_rows)
    xs = _dispatch(h, slots, zero_tiles, tile_expert.shape[0] * tile_rows, tile_rows)
    y = _grouped_ffn(xs, vec(norm_ffn_g[1]), moe_w_gate[0].astype(F32), moe_w_up[0].astype(F32),
                     moe_w_down[0].astype(F32), tile_expert, n_used, tile_rows=tile_rows)
    out = _combine(h, route, slots, y, vec(final_norm_g))
    return out.reshape(b, s, d)
```

```python
import functools

import jax
import jax.numpy as jnp
from jax import lax
from jax.experimental import pallas as pl
from jax.experimental.pallas import tpu as pltpu

F32 = jnp.float32
BF16 = jnp.bfloat16

RMS_EPS = 1e-6
POOL_WINDOWS = (2, 4, 8, 16)
DN_HEAD_DIM = 128
DN_CHUNK = 64
TOP_K = 2

LANES = 128
SUBLANES = 8
VMEM_LIMIT_BYTES = 56 * 1024 * 1024

POOL_HALO = 16
CONV_HALO = SUBLANES

MOE_TILE_ROWS = 512
ROW_GROUP = SUBLANES
WEIGHT_CHUNKS = 16
WEIGHT_SLOTS = 6
RECORD_ROWS = 16


def _dot(a, b):
    return jnp.dot(a, b, preferred_element_type=F32)


def _dot_f32(a, b):
    return jnp.dot(a, b, preferred_element_type=F32, precision=lax.Precision.HIGHEST)


def _dot_nt(a, b):
    return lax.dot_general(a, b, (((1,), (1,)), ((), ())), preferred_element_type=F32)


def _dot_tn(a, b):
    return lax.dot_general(a, b, (((0,), (0,)), ((), ())), preferred_element_type=F32)


def _rms(x, g):
    return x * lax.rsqrt(jnp.mean(x * x, axis=-1, keepdims=True) + RMS_EPS) * g


def _silu(x):
    return x * jax.nn.sigmoid(x)


def _pick_tile(n, target, quantum):
    if n <= target:
        return n
    best = None
    for t in range(quantum, target + 1, quantum):
        if n % t == 0:
            best = t
    assert best is not None, (n, target, quantum)
    return best


def _params(semantics):
    return pltpu.CompilerParams(dimension_semantics=semantics, vmem_limit_bytes=VMEM_LIMIT_BYTES)


def _resident(shape):
    nd = len(shape)
    return pl.BlockSpec(shape, lambda *_: (0,) * nd, pipeline_mode=pl.Buffered(1))


def _layer0_kernel(x_ref, xh_ref, gm_ref, win_ref, wgrp_ref, sc_ref, gf_ref, wg_ref, wu_ref, wd_ref,
                   o_ref, ext_ref, h_ref, *, tiles_per_seq):
    tm = x_ref.shape[0]
    j = pl.program_id(0) % tiles_per_seq
    x = x_ref[...]
    gm = gm_ref[...]
    win = win_ref[...]
    u = _dot(_rms(x, gm).astype(BF16), win)
    uh = _dot(_rms(xh_ref[...], gm).astype(BF16), win)
    ext_ref[0:POOL_HALO, :] = jnp.where(j == 0, 0.0, uh)
    ext_ref[POOL_HALO:, :] = u
    pos = (j * tm + lax.broadcasted_iota(jnp.int32, (tm, 1), 0) + 1).astype(F32)
    c = u.shape[1] // len(POOL_WINDOWS)
    for gi, w in enumerate(POOL_WINDOWS):
        sl = slice(gi * c, (gi + 1) * c)
        s = ext_ref[:, sl]
        sh = 1
        while sh < w:
            s = s + pltpu.roll(s, sh, 0)
            sh *= 2
        m = s[POOL_HALO:] / jnp.minimum(pos, float(w)) - u[:, sl]
        y = _dot(m.astype(BF16), wgrp_ref[gi])
        h_ref[:, sl] = x[:, sl] + y * sc_ref[:, sl]
    h = h_ref[...]
    hn = _rms(h, gf_ref[...]).astype(BF16)
    act = _silu(_dot(hn, wg_ref[...])) * _dot(hn, wu_ref[...])
    o_ref[...] = h + _dot(act.astype(BF16), wd_ref[...])


def _layer0(x2d, seq, g_mix, w_in, w_group, scale, g_ffn, w_gate, w_up, w_down):
    t, d = x2d.shape
    assert all(w & (w - 1) == 0 and w <= POOL_HALO for w in POOL_WINDOWS)
    tm = _pick_tile(seq, 512, POOL_HALO)
    assert tm % POOL_HALO == 0
    halo_blocks = tm // POOL_HALO
    row = pl.BlockSpec((tm, d), lambda i: (i, 0))
    return pl.pallas_call(
        functools.partial(_layer0_kernel, tiles_per_seq=seq // tm),
        grid=(t // tm,),
        in_specs=[
            row,
            pl.BlockSpec((POOL_HALO, d), lambda i: (jnp.maximum(i * halo_blocks - 1, 0), 0)),
            _resident((1, d)), _resident(w_in.shape), _resident(w_group.shape), _resident((1, d)),
            _resident((1, d)), _resident(w_gate.shape), _resident(w_up.shape), _resident(w_down.shape),
        ],
        out_specs=row,
        out_shape=jax.ShapeDtypeStruct((t, d), F32),
        scratch_shapes=[pltpu.VMEM((tm + POOL_HALO, d), F32), pltpu.VMEM((tm, d), F32)],
        compiler_params=_params(("parallel",)),
        name="layer0_pool_ffn",
    )(x2d, x2d, g_mix, w_in, w_group, scale, g_ffn, w_gate, w_up, w_down)


def _dn_in_kernel(h_ref, hh_ref, g_ref, wqkv_ref, wz_ref, wsm_ref, cw_ref, alog_ref, dtb_ref,
                  q_ref, k_ref, v_ref, z_ref, bg_ref, bgt_ref, ext_ref, *, tiles_per_seq):
    j = pl.program_id(0) % tiles_per_seq
    g = g_ref[...]
    hn = _rms(h_ref[...], g).astype(BF16)
    hnh = jnp.where(j == 0, 0.0, _rms(hh_ref[...], g)).astype(BF16)
    kd = q_ref.shape[1]
    heads = kd // DN_HEAD_DIM
    taps = cw_ref.shape[0]
    for part, out_ref in enumerate((q_ref, k_ref, v_ref)):
        cols = slice(part * kd, (part + 1) * kd)
        w = wqkv_ref[:, cols]
        ext_ref[0:CONV_HALO, :] = _dot(hnh, w)
        ext_ref[CONV_HALO:, :] = _dot(hn, w)
        p = ext_ref[...]
        cw = cw_ref[:, cols]
        y = p * cw[taps - 1:taps, :]
        for s in range(1, taps):
            y = y + pltpu.roll(p, s, 0) * cw[taps - 1 - s:taps - s, :]
        y = _silu(y[CONV_HALO:])
        if part == 2:
            out_ref[...] = y
            continue
        for h in range(heads):
            hs = slice(h * DN_HEAD_DIM, (h + 1) * DN_HEAD_DIM)
            yh = y[:, hs]
            yh = yh * lax.rsqrt(jnp.sum(yh * yh, axis=-1, keepdims=True) + RMS_EPS)
            if part == 0:
                yh = yh * (DN_HEAD_DIM ** -0.5)
            out_ref[:, hs] = yh
    z_ref[...] = _dot(hn, wz_ref[...])
    sm = _dot(hn, wsm_ref[...])
    xs = sm + dtb_ref[...]
    softplus = jnp.maximum(xs, 0.0) + jnp.log1p(jnp.exp(-jnp.abs(xs)))
    lane = lax.broadcasted_iota(jnp.int32, sm.shape, 1)
    bg = jnp.where(lane < heads, jax.nn.sigmoid(sm), -jnp.exp(alog_ref[...]) * softplus)
    bg_ref[...] = bg
    bgt_ref[...] = bg.T[:bgt_ref.shape[0]]


def _dn_in_proj(h, seq, g, w_qkv, w_z, w_small, conv_w, a_log_pad, dt_bias_pad):
    t, d = h.shape
    kd = w_z.shape[1]
    assert conv_w.shape[0] - 1 <= CONV_HALO
    tm = _pick_tile(seq, 512, CONV_HALO)
    halo_blocks = tm // CONV_HALO
    row = lambda n: pl.BlockSpec((tm, n), lambda i: (i, 0))
    return pl.pallas_call(
        functools.partial(_dn_in_kernel, tiles_per_seq=seq // tm),
        grid=(t // tm,),
        in_specs=[
            row(d),
            pl.BlockSpec((CONV_HALO, d), lambda i: (jnp.maximum(i * halo_blocks - 1, 0), 0)),
            _resident((1, d)),
            _resident(w_qkv.shape),
            _resident(w_z.shape),
            _resident(w_small.shape),
            _resident(conv_w.shape),
            _resident((1, LANES)),
            _resident((1, LANES)),
        ],
        out_specs=[row(kd), row(kd), row(kd), row(kd), row(LANES),
                   pl.BlockSpec((RECORD_ROWS, tm), lambda i: (0, i))],
        out_shape=[jax.ShapeDtypeStruct((t, kd), F32)] * 4 + [jax.ShapeDtypeStruct((t, LANES), F32),
                                                              jax.ShapeDtypeStruct((RECORD_ROWS, t), F32)],
        scratch_shapes=[pltpu.VMEM((tm + CONV_HALO, kd), F32)],
        compiler_params=_params(("parallel",)),
        name="deltanet_in_proj",
    )(h, h, g, w_qkv, w_z, w_small, conv_w, a_log_pad, dt_bias_pad)


def _delta_kernel(q_ref, k_ref, v_ref, z_ref, bg_ref, gt_ref, ng_ref, o_ref, state_ref):
    nb, c, kd = q_ref.shape
    heads = kd // DN_HEAD_DIM
    pairs = [(b, h) for b in range(nb) for h in range(heads)]

    @pl.when(pl.program_id(1) == 0)
    def _():
        state_ref[...] = jnp.zeros_like(state_ref)

    def head_cols(h):
        return slice(h * DN_HEAD_DIM, (h + 1) * DN_HEAD_DIM)

    row = lax.broadcasted_iota(jnp.int32, (c, c), 0)
    col = lax.broadcasted_iota(jnp.int32, (c, c), 1)
    causal = row >= col
    strict = row > col
    diag = row == col
    lower_ones = causal.astype(F32)
    upper_ones = (row <= col).astype(F32)
    bg = [bg_ref[b] for b in range(nb)]
    g_cum_col = [_dot_f32(lower_ones, bg[b]) for b in range(nb)]
    g_cum_row = [_dot_f32(gt_ref[b, 0], upper_ones) for b in range(nb)]

    q = [q_ref[b, :, head_cols(h)] for b, h in pairs]
    k = [k_ref[b, :, head_cols(h)] for b, h in pairs]
    beta = [bg[b][:, h:h + 1] for b, h in pairs]
    gc = [g_cum_col[b][:, heads + h:heads + h + 1] for b, h in pairs]
    gr = [g_cum_row[b][h:h + 1, :] for b, h in pairs]
    n = len(pairs)
    rng = range(n)
    decay = [jnp.where(causal, jnp.exp(jnp.where(causal, gc[i] - gr[i], 0.0)), 0.0) for i in rng]
    k_beta = [k[i] * beta[i] for i in rng]
    scores = [_dot_nt(jnp.concatenate([k_beta[i], q[i]], axis=0).astype(BF16), k[i].astype(BF16))
              for i in rng]
    qk = [(scores[i][c:] * decay[i]).astype(BF16) for i in rng]
    x = [jnp.where(strict, -(scores[i][:c] * decay[i]), 0.0) for i in rng]
    inv = [jnp.where(diag, 1.0, x[i]) for i in rng]
    p = 1
    while 2 * p < c:
        x16 = [x[i].astype(BF16) for i in rng]
        x = [_dot(x16[i], x16[i]) for i in rng]
        inv = [inv[i] + _dot(inv[i].astype(BF16), x[i].astype(BF16)) for i in rng]
        p *= 2
    e_gc = [jnp.exp(gc[i]) for i in rng]
    rhs = [jnp.concatenate([v_ref[b, :, head_cols(h)] * beta[i], k_beta[i] * e_gc[i]], axis=1).astype(BF16)
           for i, (b, h) in enumerate(pairs)]
    uw = [_dot(inv[i].astype(BF16), rhs[i]) for i in rng]
    state = [state_ref[i] for i in rng]
    ws_qs = [_dot(jnp.concatenate([uw[i][:, DN_HEAD_DIM:], q[i] * e_gc[i]], axis=0).astype(BF16),
                  state[i].astype(BF16)) for i in rng]
    v_new = [(uw[i][:, :DN_HEAD_DIM] - ws_qs[i][:c]).astype(BF16) for i in rng]
    o = [ws_qs[i][c:] + _dot(qk[i], v_new[i]) for i in rng]
    g_last = [gc[i][c - 1:c, :] for i in rng]
    k_dec = [(k[i] * jnp.exp(g_last[i] - gc[i])).astype(BF16) for i in rng]
    for i in rng:
        state_ref[i] = state[i] * jnp.exp(g_last[i]) + _dot_tn(k_dec[i], v_new[i])
    ng = ng_ref[...]
    for i, (b, h) in enumerate(pairs):
        on = o[i] * lax.rsqrt(jnp.mean(o[i] * o[i], axis=-1, keepdims=True) + RMS_EPS) * ng
        o_ref[b, :, head_cols(h)] = (on * _silu(z_ref[b, :, head_cols(h)])).astype(BF16)


def _delta_rule(q, k, v, z, bg, g_rows, norm_g, *, seqs_per_step):
    b, s, kd = q.shape
    heads = kd // DN_HEAD_DIM
    c = DN_CHUNK
    nb = seqs_per_step
    assert s % c == 0 and c & (c - 1) == 0 and b % nb == 0
    blk = lambda n: pl.BlockSpec((nb, c, n), lambda i, j: (i, j, 0))
    return pl.pallas_call(
        _delta_kernel,
        grid=(b // nb, s // c),
        in_specs=[blk(kd), blk(kd), blk(kd), blk(kd), blk(LANES),
                  pl.BlockSpec((nb, 1, heads, c), lambda i, j: (i, j, 0, 0)),
                  pl.BlockSpec((1, DN_HEAD_DIM), lambda i, j: (0, 0))],
        out_specs=blk(kd),
        out_shape=jax.ShapeDtypeStruct((b, s, kd), BF16),
        scratch_shapes=[pltpu.VMEM((nb * heads, DN_HEAD_DIM, DN_HEAD_DIM), F32)],
        compiler_params=_params(("parallel", "arbitrary")),
        name="delta_rule",
    )(q, k, v, z, bg, g_rows, norm_g)


def _split_bf16(a):
    hi = a.astype(BF16)
    return hi, (a - hi.astype(F32)).astype(BF16)


def _out_router_kernel(h_ref, o_ref, wo_ref, g_ref, rw_hi_ref, rw_lo_ref, rb_ref, h_out_ref, route_ref, route_t_ref):
    h = h_ref[...] + _dot(o_ref[...], wo_ref[...])
    h_out_ref[...] = h
    hn_hi, hn_lo = _split_bf16(_rms(h, g_ref[...]))
    rw_hi = rw_hi_ref[...]
    logits = _dot(hn_hi, rw_hi) + (_dot(hn_lo, rw_hi) + _dot(hn_hi, rw_lo_ref[...]))
    logits = logits + rb_ref[...]
    lane = lax.broadcasted_iota(jnp.int32, logits.shape, 1)
    remaining = logits
    tops, picks = [], []
    for _ in range(TOP_K):
        top = jnp.max(remaining, axis=1, keepdims=True)
        pick = jnp.min(jnp.where(remaining == top, lane, LANES), axis=1, keepdims=True)
        tops.append(top)
        picks.append(pick)
        remaining = jnp.where(lane == pick, -jnp.inf, remaining)
    exps = [jnp.exp(top - tops[0]) for top in tops]
    denom = sum(exps)
    route = jnp.zeros_like(logits)
    for j, (pick, ex) in enumerate(zip(picks, exps)):
        route = jnp.where(lane == j, pick.astype(F32), route)
        route = jnp.where(lane == TOP_K + j, ex / denom, route)
    route_ref[...] = route
    route_t_ref[...] = route.T[:route_t_ref.shape[0]]


def _out_proj_router(h, o, w_out, g, router_w_pad, router_b_pad):
    rw_hi, rw_lo = _split_bf16(router_w_pad)
    t, d = h.shape
    tm = _pick_tile(t, 512, SUBLANES)
    row = lambda n: pl.BlockSpec((tm, n), lambda i: (i, 0))
    return pl.pallas_call(
        _out_router_kernel,
        grid=(t // tm,),
        in_specs=[row(d), row(o.shape[1]), _resident(w_out.shape), _resident((1, d)),
                  _resident(router_w_pad.shape), _resident(router_w_pad.shape), _resident((1, LANES))],
        out_specs=[row(d), row(LANES), pl.BlockSpec((RECORD_ROWS, tm), lambda i: (0, i))],
        out_shape=[jax.ShapeDtypeStruct((t, d), F32), jax.ShapeDtypeStruct((t, LANES), F32),
                   jax.ShapeDtypeStruct((RECORD_ROWS, t), F32)],
        compiler_params=_params(("parallel",)),
        name="out_proj_router",
    )(h, o, w_out, g, rw_hi, rw_lo, router_b_pad)


def _routing_tables(route_t, n_experts, tile_rows):
    t = route_t.shape[1]
    assert (TOP_K * t) % tile_rows == 0
    idx = route_t[:TOP_K].astype(jnp.int32)
    experts = jnp.arange(n_experts, dtype=jnp.int32)[:, None]
    hits = [idx[j][None, :] == experts for j in range(TOP_K)]
    member = functools.reduce(jnp.logical_or, hits).astype(jnp.int32)
    counts = member.sum(axis=1)
    rank = jnp.cumsum(member, axis=1) - member
    padded = (counts + tile_rows - 1) // tile_rows * tile_rows
    ends = jnp.cumsum(padded)
    starts = ends - padded
    row_of = starts[:, None] + rank
    slots = [jnp.sum(jnp.where(hit, row_of, 0), axis=0).astype(jnp.int32) for hit in hits]
    n_tiles = TOP_K * t // tile_rows + n_experts
    n_used = (ends[-1] // tile_rows).astype(jnp.int32)
    tile_start = jnp.arange(n_tiles, dtype=jnp.int32) * tile_rows
    tile_expert = jnp.sum(tile_start[:, None] >= ends[None, :], axis=1).astype(jnp.int32)
    tile_expert = jnp.where(jnp.arange(n_tiles) < n_used, tile_expert, tile_expert[n_used - 1])
    last_tile = jnp.where(padded > 0, ends // tile_rows - 1, -1)
    trailing = n_used + jnp.arange(n_experts, dtype=jnp.int32)
    trailing = jnp.where(trailing < n_tiles, trailing, -1)
    zero_tiles = jnp.concatenate([last_tile, trailing]).astype(jnp.int32)
    return slots, tile_expert, n_used.reshape(1), zero_tiles


def _row_copy(src_ref, src_row, dst_ref, dst_row, sem):
    return pltpu.make_async_copy(src_ref.at[pl.ds(src_row, 1)], dst_ref.at[pl.ds(dst_row, 1)], sem)


def _dispatch_kernel(zero_tiles_ref, *refs):
    slot_refs, (h_ref, xs_hbm, zeros_ref, sem) = refs[:TOP_K], refs[TOP_K:]
    tm = h_ref.shape[0]
    tile_rows = zeros_ref.shape[0]

    @pl.when(pl.program_id(0) == 0)
    def _():
        zeros_ref[...] = jnp.zeros_like(zeros_ref)

        def fill(j):
            row = pl.multiple_of(jnp.maximum(zero_tiles_ref[j], 0) * tile_rows, tile_rows)
            return pltpu.make_async_copy(zeros_ref, xs_hbm.at[pl.ds(row, tile_rows)], sem)

        for j in range(zero_tiles_ref.shape[0]):
            pl.when(zero_tiles_ref[j] >= 0)(fill(j).start)
        for j in range(zero_tiles_ref.shape[0]):
            pl.when(zero_tiles_ref[j] >= 0)(fill(j).wait)

    @pl.loop(0, tm // ROW_GROUP)
    def _(g):
        base = pl.multiple_of(g * ROW_GROUP, ROW_GROUP)
        rows = h_ref.at[pl.ds(base, ROW_GROUP)]
        for u in range(ROW_GROUP):
            for j in range(TOP_K):
                _row_copy(rows, u, xs_hbm, slot_refs[j][base + u], sem).start(priority=j % 2)

    for _ in range(TOP_K):
        pltpu.make_async_copy(h_ref, xs_hbm.at[pl.ds(0, tm)], sem).wait()


def _dispatch(h, slots, zero_tiles, n_rows, tile_rows):
    t, d = h.shape
    tm = _pick_tile(t, 1024, SUBLANES)
    return pl.pallas_call(
        _dispatch_kernel,
        grid=(t // tm,),
        in_specs=[pl.BlockSpec(zero_tiles.shape, lambda i: (0,), memory_space=pltpu.SMEM)]
        + [pl.BlockSpec((tm,), lambda i: (i,), memory_space=pltpu.SMEM)] * TOP_K
        + [pl.BlockSpec((tm, d), lambda i: (i, 0))],
        out_specs=pl.BlockSpec(memory_space=pl.ANY),
        out_shape=jax.ShapeDtypeStruct((n_rows, d), F32),
        scratch_shapes=[pltpu.VMEM((tile_rows, d), F32), pltpu.SemaphoreType.DMA(())],
        compiler_params=_params(("arbitrary",)),
        name="moe_dispatch",
    )(zero_tiles, *slots, h)


def _grouped_ffn_kernel(te_ref, nu_ref, x_ref, g_ref, wg_hbm, wu_hbm, wd_hbm, y_ref,
                        wg_ref, wu_ref, wd_ref, stage_in_ref, stage_out_ref, sems):
    i = pl.program_id(0)
    used = i < nu_ref[0]
    expert = te_ref[i]

    @pl.when(jnp.logical_not(used))
    def _():
        y_ref[...] = jnp.zeros_like(y_ref)

    @pl.when(used & ((i == 0) | (te_ref[jnp.maximum(i - 1, 0)] != expert)))
    def _():
        chunks = []
        for src, stage, dst in ((wg_hbm, stage_in_ref, wg_ref), (wu_hbm, stage_in_ref, wu_ref),
                                (wd_hbm, stage_out_ref, wd_ref)):
            rows = stage.shape[1]
            for c in range(dst.shape[0] // rows):
                chunks.append((src.at[expert, pl.ds(c * rows, rows)], stage, dst.at[pl.ds(c * rows, rows)]))

        slots = sems.shape[0]

        def copy(n):
            src, stage, _ = chunks[n]
            return pltpu.make_async_copy(src, stage.at[n % slots], sems.at[n % slots])

        for n in range(slots - 1):
            copy(n).start()
        for n in range(len(chunks)):
            copy(n).wait()
            ahead = n + slots - 1
            if ahead < len(chunks):
                copy(ahead).start()
            _, stage, dst = chunks[n]
            dst[...] = stage[n % slots].astype(BF16)

    @pl.when(used)
    def _():
        xn = _rms(x_ref[...], g_ref[...]).astype(BF16)
        act = _silu(_dot(xn, wg_ref[...])) * _dot(xn, wu_ref[...])
        y_ref[...] = _dot(act.astype(BF16), wd_ref[...])


def _grouped_ffn(xs, g, w_gate, w_up, w_down, tile_expert, n_used, *, tile_rows):
    n_rows, d = xs.shape
    ff = w_gate.shape[2]
    assert d % WEIGHT_CHUNKS == 0 and ff % WEIGHT_CHUNKS == 0

    def x_map(i, te, nu):
        return (jnp.minimum(i, nu[0] - 1), 0)

    return pl.pallas_call(
        _grouped_ffn_kernel,
        grid_spec=pltpu.PrefetchScalarGridSpec(
            num_scalar_prefetch=2,
            grid=(n_rows // tile_rows,),
            in_specs=[
                pl.BlockSpec((tile_rows, d), x_map),
                pl.BlockSpec((1, d), lambda i, te, nu: (0, 0)),
                pl.BlockSpec(memory_space=pl.ANY),
                pl.BlockSpec(memory_space=pl.ANY),
                pl.BlockSpec(memory_space=pl.ANY),
            ],
            out_specs=pl.BlockSpec((tile_rows, d), lambda i, te, nu: (i, 0)),
            scratch_shapes=[
                pltpu.VMEM((d, ff), BF16), pltpu.VMEM((d, ff), BF16), pltpu.VMEM((ff, d), BF16),
                pltpu.VMEM((WEIGHT_SLOTS, d // WEIGHT_CHUNKS, ff), F32),
                pltpu.VMEM((WEIGHT_SLOTS, ff // WEIGHT_CHUNKS, d), F32),
                pltpu.SemaphoreType.DMA((WEIGHT_SLOTS,)),
            ],
        ),
        out_shape=jax.ShapeDtypeStruct((n_rows, d), F32),
        compiler_params=_params(("arbitrary",)),
        name="moe_grouped_ffn",
    )(tile_expert, n_used, xs, g, w_gate, w_up, w_down)


def _combine_kernel(*refs):
    slot_refs, (h_ref, route_ref, gf_ref, y_hbm, o_ref, ybuf_ref, sem) = refs[:TOP_K], refs[TOP_K:]
    tm = h_ref.shape[0]

    @pl.loop(0, tm // ROW_GROUP)
    def _(g):
        base = pl.multiple_of(g * ROW_GROUP, ROW_GROUP)
        for j in range(TOP_K):
            rows = ybuf_ref.at[j, pl.ds(base, ROW_GROUP)]
            for u in range(ROW_GROUP):
                _row_copy(y_hbm, slot_refs[j][base + u], rows, u, sem).start(priority=j % 2)

    for j in range(TOP_K):
        pltpu.make_async_copy(y_hbm.at[pl.ds(0, tm)], ybuf_ref.at[j], sem).wait()

    route = route_ref[...]
    out = h_ref[...]
    for j in range(TOP_K):
        out = out + route[:, TOP_K + j:TOP_K + j + 1] * ybuf_ref[j]
    o_ref[...] = _rms(out, gf_ref[...])


def _combine(h, route, slots, y, final_g):
    t, d = h.shape
    tm = _pick_tile(t, 512, SUBLANES)
    row = lambda n: pl.BlockSpec((tm, n), lambda i: (i, 0))
    return pl.pallas_call(
        _combine_kernel,
        grid=(t // tm,),
        in_specs=[pl.BlockSpec((tm,), lambda i: (i,), memory_space=pltpu.SMEM)] * TOP_K
        + [row(d), row(LANES), pl.BlockSpec((1, d), lambda i: (0, 0)), pl.BlockSpec(memory_space=pl.ANY)],
        out_specs=row(d),
        out_shape=jax.ShapeDtypeStruct((t, d), F32),
        scratch_shapes=[pltpu.VMEM((TOP_K, tm, d), F32), pltpu.SemaphoreType.DMA(())],
        compiler_params=_params(("arbitrary",)),
        name="moe_combine",
    )(*slots, h, route, final_g, y)


def _pad_lanes(a, offset=0, fill=0.0):
    n = a.shape[-1]
    pad = [(0, 0)] * (a.ndim - 1) + [(offset, LANES - offset - n)]
    return jnp.pad(a, pad, constant_values=fill)


def kernel(x, norm_mix_g, norm_ffn_g, pool_w_in, pool_w_group, pool_scale, dn_w_in, dn_conv_w, dn_a_log, dn_dt_bias, dn_norm_g, dn_w_out, ffn_w_gate, ffn_w_up, ffn_w_down, moe_router_w, moe_router_b, moe_w_gate, moe_w_up, moe_w_down, final_norm_g):
    b, s, d = x.shape
    t = b * s
    kd = dn_w_out.shape[1]
    heads = kd // DN_HEAD_DIM
    n_experts = moe_router_w.shape[-1]
    assert 2 * heads <= RECORD_ROWS and n_experts <= LANES and 2 * TOP_K <= RECORD_ROWS
    vec = lambda a: a.reshape(1, -1).astype(F32)

    h = x.reshape(t, d)

    h = _layer0(h, s, vec(norm_mix_g[0]), pool_w_in[0].astype(BF16), pool_w_group[0].astype(BF16),
                vec(pool_scale[0]), vec(norm_ffn_g[0]), ffn_w_gate[0].astype(BF16), ffn_w_up[0].astype(BF16),
                ffn_w_down[0].astype(BF16))

    w_in = dn_w_in[0]
    w_small = _pad_lanes(w_in[:, 4 * kd:]).astype(BF16)
    q, k, v, z, bg, bg_t = _dn_in_proj(
        h, s, vec(norm_mix_g[1]), w_in[:, :3 * kd].astype(BF16), w_in[:, 3 * kd:4 * kd].astype(BF16), w_small,
        dn_conv_w[0].astype(F32), _pad_lanes(vec(dn_a_log[0]), heads), _pad_lanes(vec(dn_dt_bias[0]), heads))
    g_rows = bg_t[heads:2 * heads].reshape(heads, b, s // DN_CHUNK, DN_CHUNK).transpose(1, 2, 0, 3)
    shape3 = lambda a: a.reshape(b, s, a.shape[-1])
    o = _delta_rule(shape3(q), shape3(k), shape3(v), shape3(z), shape3(bg), g_rows, vec(dn_norm_g[0]),
                    seqs_per_step=4 if b % 4 == 0 else (2 if b % 2 == 0 else 1))
    h, route, route_t = _out_proj_router(
        h, o.reshape(t, kd), dn_w_out[0].astype(BF16), vec(norm_ffn_g[1]),
        _pad_lanes(moe_router_w[0].astype(F32)), _pad_lanes(vec(moe_router_b[0]), fill=-jnp.inf))

    tile_rows = _pick_tile(TOP_K * t, MOE_TILE_ROWS, SUBLANES)
    slots, tile_expert, n_used, zero_tiles = _routing_tables(route_t, n_experts, tile_rows)
    xs = _dispatch(h, slots, zero_tiles, tile_expert.shape[0] * tile_rows, tile_rows)
    y = _grouped_ffn(xs, vec(norm_ffn_g[1]), moe_w_gate[0].astype(F32), moe_w_up[0].astype(F32),
                     moe_w_down[0].astype(F32), tile_expert, n_used, tile_rows=tile_rows)
    out = _combine(h, route, slots, y, vec(final_norm_g))
    return out.reshape(b, s, d)
```

```python
import functools

import jax
import jax.numpy as jnp
from jax import lax
from jax.experimental import pallas as pl
from jax.experimental.pallas import tpu as pltpu

F32 = jnp.float32
BF16 = jnp.bfloat16

RMS_EPS = 1e-6
POOL_WINDOWS = (2, 4, 8, 16)
DN_HEAD_DIM = 128
DN_CHUNK = 64
TOP_K = 2

LANES = 128
SUBLANES = 8
VMEM_LIMIT_BYTES = 56 * 1024 * 1024

POOL_HALO = 16
CONV_HALO = SUBLANES

MOE_TILE_ROWS = 512
ROW_GROUP = SUBLANES
WEIGHT_CHUNKS = 16
WEIGHT_SLOTS = 6
RECORD_ROWS = 16


def _dot(a, b):
    return jnp.dot(a, b, preferred_element_type=F32)


def _dot_f32(a, b):
    return jnp.dot(a, b, preferred_element_type=F32, precision=lax.Precision.HIGHEST)


def _dot_nt(a, b):
    return lax.dot_general(a, b, (((1,), (1,)), ((), ())), preferred_element_type=F32)


def _dot_tn(a, b):
    return lax.dot_general(a, b, (((0,), (0,)), ((), ())), preferred_element_type=F32)


def _rms(x, g):
    return x * lax.rsqrt(jnp.mean(x * x, axis=-1, keepdims=True) + RMS_EPS) * g


def _silu(x):
    return x * jax.nn.sigmoid(x)


def _pick_tile(n, target, quantum):
    if n <= target:
        return n
    best = None
    for t in range(quantum, target + 1, quantum):
        if n % t == 0:
            best = t
    assert best is not None, (n, target, quantum)
    return best


def _params(semantics):
    return pltpu.CompilerParams(dimension_semantics=semantics, vmem_limit_bytes=VMEM_LIMIT_BYTES)


def _resident(shape):
    nd = len(shape)
    return pl.BlockSpec(shape, lambda *_: (0,) * nd, pipeline_mode=pl.Buffered(1))


def _layer0_kernel(x_ref, xh_ref, gm_ref, win_ref, wgrp_ref, sc_ref, gf_ref, wg_ref, wu_ref, wd_ref,
                   o_ref, ext_ref, h_ref, *, tiles_per_seq):
    tm = x_ref.shape[0]
    j = pl.program_id(0) % tiles_per_seq
    x = x_ref[...]
    gm = gm_ref[...]
    win = win_ref[...]
    u = _dot(_rms(x, gm).astype(BF16), win)
    uh = _dot(_rms(xh_ref[...], gm).astype(BF16), win)
    ext_ref[0:POOL_HALO, :] = jnp.where(j == 0, 0.0, uh)
    ext_ref[POOL_HALO:, :] = u
    pos = (j * tm + lax.broadcasted_iota(jnp.int32, (tm, 1), 0) + 1).astype(F32)
    c = u.shape[1] // len(POOL_WINDOWS)
    for gi, w in enumerate(POOL_WINDOWS):
        sl = slice(gi * c, (gi + 1) * c)
        s = ext_ref[:, sl]
        sh = 1
        while sh < w:
            s = s + pltpu.roll(s, sh, 0)
            sh *= 2
        m = s[POOL_HALO:] / jnp.minimum(pos, float(w)) - u[:, sl]
        y = _dot(m.astype(BF16), wgrp_ref[gi])
        h_ref[:, sl] = x[:, sl] + y * sc_ref[:, sl]
    h = h_ref[...]
    hn = _rms(h, gf_ref[...]).astype(BF16)
    act = _silu(_dot(hn, wg_ref[...])) * _dot(hn, wu_ref[...])
    o_ref[...] = h + _dot(act.astype(BF16), wd_ref[...])


def _layer0(x2d, seq, g_mix, w_in, w_group, scale, g_ffn, w_gate, w_up, w_down):
    t, d = x2d.shape
    assert all(w & (w - 1) == 0 and w <= POOL_HALO for w in POOL_WINDOWS)
    tm = _pick_tile(seq, 512, POOL_HALO)
    assert tm % POOL_HALO == 0
    halo_blocks = tm // POOL_HALO
    row = pl.BlockSpec((tm, d), lambda i: (i, 0))
    return pl.pallas_call(
        functools.partial(_layer0_kernel, tiles_per_seq=seq // tm),
        grid=(t // tm,),
        in_specs=[
            row,
            pl.BlockSpec((POOL_HALO, d), lambda i: (jnp.maximum(i * halo_blocks - 1, 0), 0)),
            _resident((1, d)), _resident(w_in.shape), _resident(w_group.shape), _resident((1, d)),
            _resident((1, d)), _resident(w_gate.shape), _resident(w_up.shape), _resident(w_down.shape),
        ],
        out_specs=row,
        out_shape=jax.ShapeDtypeStruct((t, d), F32),
        scratch_shapes=[pltpu.VMEM((tm + POOL_HALO, d), F32), pltpu.VMEM((tm, d), F32)],
        compiler_params=_params(("parallel",)),
        name="layer0_pool_ffn",
    )(x2d, x2d, g_mix, w_in, w_group, scale, g_ffn, w_gate, w_up, w_down)


def _dn_in_kernel(h_ref, hh_ref, g_ref, wqkv_ref, wz_ref, wsm_ref, cw_ref, alog_ref, dtb_ref,
                  q_ref, k_ref, v_ref, z_ref, bg_ref, bgt_ref, ext_ref, *, tiles_per_seq):
    j = pl.program_id(0) % tiles_per_seq
    g = g_ref[...]
    hn = _rms(h_ref[...], g).astype(BF16)
    hnh = jnp.where(j == 0, 0.0, _rms(hh_ref[...], g)).astype(BF16)
    kd = q_ref.shape[1]
    heads = kd // DN_HEAD_DIM
    taps = cw_ref.shape[0]
    for part, out_ref in enumerate((q_ref, k_ref, v_ref)):
        cols = slice(part * kd, (part + 1) * kd)
        w = wqkv_ref[:, cols]
        ext_ref[0:CONV_HALO, :] = _dot(hnh, w)
        ext_ref[CONV_HALO:, :] = _dot(hn, w)
        p = ext_ref[...]
        cw = cw_ref[:, cols]
        y = p * cw[taps - 1:taps, :]
        for s in range(1, taps):
            y = y + pltpu.roll(p, s, 0) * cw[taps - 1 - s:taps - s, :]
        y = _silu(y[CONV_HALO:])
        if part == 2:
            out_ref[...] = y
            continue
        for h in range(heads):
            hs = slice(h * DN_HEAD_DIM, (h + 1) * DN_HEAD_DIM)
            yh = y[:, hs]
            yh = yh * lax.rsqrt(jnp.sum(yh * yh, axis=-1, keepdims=True) + RMS_EPS)
            if part == 0:
                yh = yh * (DN_HEAD_DIM ** -0.5)
            out_ref[:, hs] = yh
    z_ref[...] = _dot(hn, wz_ref[...])
    sm = _dot(hn, wsm_ref[...])
    xs = sm + dtb_ref[...]
    softplus = jnp.maximum(xs, 0.0) + jnp.log1p(jnp.exp(-jnp.abs(xs)))
    lane = lax.broadcasted_iota(jnp.int32, sm.shape, 1)
    bg = jnp.where(lane < heads, jax.nn.sigmoid(sm), -jnp.exp(alog_ref[...]) * softplus)
    bg_ref[...] = bg
    bgt_ref[...] = bg.T[:bgt_ref.shape[0]]


def _dn_in_proj(h, seq, g, w_qkv, w_z, w_small, conv_w, a_log_pad, dt_bias_pad):
    t, d = h.shape
    kd = w_z.shape[1]
    assert conv_w.shape[0] - 1 <= CONV_HALO
    tm = _pick_tile(seq, 512, CONV_HALO)
    halo_blocks = tm // CONV_HALO
    row = lambda n: pl.BlockSpec((tm, n), lambda i: (i, 0))
    return pl.pallas_call(
        functools.partial(_dn_in_kernel, tiles_per_seq=seq // tm),
        grid=(t // tm,),
        in_specs=[
            row(d),
            pl.BlockSpec((CONV_HALO, d), lambda i: (jnp.maximum(i * halo_blocks - 1, 0), 0)),
            _resident((1, d)),
            _resident(w_qkv.shape),
            _resident(w_z.shape),
            _resident(w_small.shape),
            _resident(conv_w.shape),
            _resident((1, LANES)),
            _resident((1, LANES)),
        ],
        out_specs=[row(kd), row(kd), row(kd), row(kd), row(LANES),
                   pl.BlockSpec((RECORD_ROWS, tm), lambda i: (0, i))],
        out_shape=[jax.ShapeDtypeStruct((t, kd), F32)] * 4 + [jax.ShapeDtypeStruct((t, LANES), F32),
                                                              jax.ShapeDtypeStruct((RECORD_ROWS, t), F32)],
        scratch_shapes=[pltpu.VMEM((tm + CONV_HALO, kd), F32)],
        compiler_params=_params(("parallel",)),
        name="deltanet_in_proj",
    )(h, h, g, w_qkv, w_z, w_small, conv_w, a_log_pad, dt_bias_pad)


def _delta_kernel(q_ref, k_ref, v_ref, z_ref, bg_ref, gt_ref, ng_ref, o_ref, state_ref):
    nb, c, kd = q_ref.shape
    heads = kd // DN_HEAD_DIM
    pairs = [(b, h) for b in range(nb) for h in range(heads)]

    @pl.when(pl.program_id(1) == 0)
    def _():
        state_ref[...] = jnp.zeros_like(state_ref)

    def head_cols(h):
        return slice(h * DN_HEAD_DIM, (h + 1) * DN_HEAD_DIM)

    row = lax.broadcasted_iota(jnp.int32, (c, c), 0)
    col = lax.broadcasted_iota(jnp.int32, (c, c), 1)
    causal = row >= col
    strict = row > col
    diag = row == col
    lower_ones = causal.astype(F32)
    upper_ones = (row <= col).astype(F32)
    bg = [bg_ref[b] for b in range(nb)]
    g_cum_col = [_dot_f32(lower_ones, bg[b]) for b in range(nb)]
    g_cum_row = [_dot_f32(gt_ref[b, 0], upper_ones) for b in range(nb)]

    q = [q_ref[b, :, head_cols(h)] for b, h in pairs]
    k = [k_ref[b, :, head_cols(h)] for b, h in pairs]
    beta = [bg[b][:, h:h + 1] for b, h in pairs]
    gc = [g_cum_col[b][:, heads + h:heads + h + 1] for b, h in pairs]
    gr = [g_cum_row[b][h:h + 1, :] for b, h in pairs]
    n = len(pairs)
    rng = range(n)
    decay = [jnp.where(causal, jnp.exp(jnp.where(causal, gc[i] - gr[i], 0.0)), 0.0) for i in rng]
    k_beta = [k[i] * beta[i] for i in rng]
    scores = [_dot_nt(jnp.concatenate([k_beta[i], q[i]], axis=0).astype(BF16), k[i].astype(BF16))
              for i in rng]
    qk = [(scores[i][c:] * decay[i]).astype(BF16) for i in rng]
    x = [jnp.where(strict, -(scores[i][:c] * decay[i]), 0.0) for i in rng]
    inv = [jnp.where(diag, 1.0, x[i]) for i in rng]
    p = 1
    while 2 * p < c:
        x16 = [x[i].astype(BF16) for i in rng]
        x = [_dot(x16[i], x16[i]) for i in rng]
        inv = [inv[i] + _dot(inv[i].astype(BF16), x[i].astype(BF16)) for i in rng]
        p *= 2
    e_gc = [jnp.exp(gc[i]) for i in rng]
    rhs = [jnp.concatenate([v_ref[b, :, head_cols(h)] * beta[i], k_beta[i] * e_gc[i]], axis=1).astype(BF16)
           for i, (b, h) in enumerate(pairs)]
    uw = [_dot(inv[i].astype(BF16), rhs[i]) for i in rng]
    state = [state_ref[i] for i in rng]
    ws_qs = [_dot(jnp.concatenate([uw[i][:, DN_HEAD_DIM:], q[i] * e_gc[i]], axis=0).astype(BF16),
                  state[i].astype(BF16)) for i in rng]
    v_new = [(uw[i][:, :DN_HEAD_DIM] - ws_qs[i][:c]).astype(BF16) for i in rng]
    o = [ws_qs[i][c:] + _dot(qk[i], v_new[i]) for i in rng]
    g_last = [gc[i][c - 1:c, :] for i in rng]
    k_dec = [(k[i] * jnp.exp(g_last[i] - gc[i])).astype(BF16) for i in rng]
    for i in rng:
        state_ref[i] = state[i] * jnp.exp(g_last[i]) + _dot_tn(k_dec[i], v_new[i])
    ng = ng_ref[...]
    for i, (b, h) in enumerate(pairs):
        on = o[i] * lax.rsqrt(jnp.mean(o[i] * o[i], axis=-1, keepdims=True) + RMS_EPS) * ng
        o_ref[b, :, head_cols(h)] = (on * _silu(z_ref[b, :, head_cols(h)])).astype(BF16)


def _delta_rule(q, k, v, z, bg, g_rows, norm_g, *, seqs_per_step):
    b, s, kd = q.shape
    heads = kd // DN_HEAD_DIM
    c = DN_CHUNK
    nb = seqs_per_step
    assert s % c == 0 and c & (c - 1) == 0 and b % nb == 0
    blk = lambda n: pl.BlockSpec((nb, c, n), lambda i, j: (i, j, 0))
    return pl.pallas_call(
        _delta_kernel,
        grid=(b // nb, s // c),
        in_specs=[blk(kd), blk(kd), blk(kd), blk(kd), blk(LANES),
                  pl.BlockSpec((nb, 1, heads, c), lambda i, j: (i, j, 0, 0)),
                  pl.BlockSpec((1, DN_HEAD_DIM), lambda i, j: (0, 0))],
        out_specs=blk(kd),
        out_shape=jax.ShapeDtypeStruct((b, s, kd), BF16),
        scratch_shapes=[pltpu.VMEM((nb * heads, DN_HEAD_DIM, DN_HEAD_DIM), F32)],
        compiler_params=_params(("parallel", "arbitrary")),
        name="delta_rule",
    )(q, k, v, z, bg, g_rows, norm_g)


def _split_bf16(a):
    hi = a.astype(BF16)
    return hi, (a - hi.astype(F32)).astype(BF16)


def _out_router_kernel(h_ref, o_ref, wo_ref, g_ref, rw_hi_ref, rw_lo_ref, rb_ref, h_out_ref, route_ref, route_t_ref):
    h = h_ref[...] + _dot(o_ref[...], wo_ref[...])
    h_out_ref[...] = h
    hn_hi, hn_lo = _split_bf16(_rms(h, g_ref[...]))
    rw_hi = rw_hi_ref[...]
    logits = _dot(hn_hi, rw_hi) + (_dot(hn_lo, rw_hi) + _dot(hn_hi, rw_lo_ref[...]))
    logits = logits + rb_ref[...]
    lane = lax.broadcasted_iota(jnp.int32, logits.shape, 1)
    remaining = logits
    tops, picks = [], []
    for _ in range(TOP_K):
        top = jnp.max(remaining, axis=1, keepdims=True)
        pick = jnp.min(jnp.where(remaining == top, lane, LANES), axis=1, keepdims=True)
        tops.append(top)
        picks.append(pick)
        remaining = jnp.where(lane == pick, -jnp.inf, remaining)
    exps = [jnp.exp(top - tops[0]) for top in tops]
    denom = sum(exps)
    route = jnp.zeros_like(logits)
    for j, (pick, ex) in enumerate(zip(picks, exps)):
        route = jnp.where(lane == j, pick.astype(F32), route)
        route = jnp.where(lane == TOP_K + j, ex / denom, route)
    route_ref[...] = route
    route_t_ref[...] = route.T[:route_t_ref.shape[0]]


def _out_proj_router(h, o, w_out, g, router_w_pad, router_b_pad):
    rw_hi, rw_lo = _split_bf16(router_w_pad)
    t, d = h.shape
    tm = _pick_tile(t, 512, SUBLANES)
    row = lambda n: pl.BlockSpec((tm, n), lambda i: (i, 0))
    return pl.pallas_call(
        _out_router_kernel,
        grid=(t // tm,),
        in_specs=[row(d), row(o.shape[1]), _resident(w_out.shape), _resident((1, d)),
                  _resident(router_w_pad.shape), _resident(router_w_pad.shape), _resident((1, LANES))],
        out_specs=[row(d), row(LANES), pl.BlockSpec((RECORD_ROWS, tm), lambda i: (0, i))],
        out_shape=[jax.ShapeDtypeStruct((t, d), F32), jax.ShapeDtypeStruct((t, LANES), F32),
                   jax.ShapeDtypeStruct((RECORD_ROWS, t), F32)],
        compiler_params=_params(("parallel",)),
        name="out_proj_router",
    )(h, o, w_out, g, rw_hi, rw_lo, router_b_pad)


def _routing_tables(route_t, n_experts, tile_rows):
    t = route_t.shape[1]
    assert (TOP_K * t) % tile_rows == 0
    idx = route_t[:TOP_K].astype(jnp.int32)
    experts = jnp.arange(n_experts, dtype=jnp.int32)[:, None]
    hits = [idx[j][None, :] == experts for j in range(TOP_K)]
    member = functools.reduce(jnp.logical_or, hits).astype(jnp.int32)
    counts = member.sum(axis=1)
    rank = jnp.cumsum(member, axis=1) - member
    padded = (counts + tile_rows - 1) // tile_rows * tile_rows
    ends = jnp.cumsum(padded)
    starts = ends - padded
    row_of = starts[:, None] + rank
    slots = [jnp.sum(jnp.where(hit, row_of, 0), axis=0).astype(jnp.int32) for hit in hits]
    n_tiles = TOP_K * t // tile_rows + n_experts
    n_used = (ends[-1] // tile_rows).astype(jnp.int32)
    tile_start = jnp.arange(n_tiles, dtype=jnp.int32) * tile_rows
    tile_expert = jnp.sum(tile_start[:, None] >= ends[None, :], axis=1).astype(jnp.int32)
    tile_expert = jnp.where(jnp.arange(n_tiles) < n_used, tile_expert, tile_expert[n_used - 1])
    last_tile = jnp.where(padded > 0, ends // tile_rows - 1, -1)
    trailing = n_used + jnp.arange(n_experts, dtype=jnp.int32)
    trailing = jnp.where(trailing < n_tiles, trailing, -1)
    zero_tiles = jnp.concatenate([last_tile, trailing]).astype(jnp.int32)
    return slots, tile_expert, n_used.reshape(1), zero_tiles


def _row_copy(src_ref, src_row, dst_ref, dst_row, sem):
    return pltpu.make_async_copy(src_ref.at[pl.ds(src_row, 1)], dst_ref.at[pl.ds(dst_row, 1)], sem)


def _dispatch_kernel(zero_tiles_ref, *refs):
    slot_refs, (h_ref, xs_hbm, zeros_ref, sem) = refs[:TOP_K], refs[TOP_K:]
    tm = h_ref.shape[0]
    tile_rows = zeros_ref.shape[0]

    @pl.when(pl.program_id(0) == 0)
    def _():
        zeros_ref[...] = jnp.zeros_like(zeros_ref)

        def fill(j):
            row = pl.multiple_of(jnp.maximum(zero_tiles_ref[j], 0) * tile_rows, tile_rows)
            return pltpu.make_async_copy(zeros_ref, xs_hbm.at[pl.ds(row, tile_rows)], sem)

        for j in range(zero_tiles_ref.shape[0]):
            pl.when(zero_tiles_ref[j] >= 0)(fill(j).start)
        for j in range(zero_tiles_ref.shape[0]):
            pl.when(zero_tiles_ref[j] >= 0)(fill(j).wait)

    @pl.loop(0, tm // ROW_GROUP)
    def _(g):
        base = pl.multiple_of(g * ROW_GROUP, ROW_GROUP)
        rows = h_ref.at[pl.ds(base, ROW_GROUP)]
        for u in range(ROW_GROUP):
            for j in range(TOP_K):
                _row_copy(rows, u, xs_hbm, slot_refs[j][base + u], sem).start(priority=j % 2)

    for _ in range(TOP_K):
        pltpu.make_async_copy(h_ref, xs_hbm.at[pl.ds(0, tm)], sem).wait()


def _dispatch(h, slots, zero_tiles, n_rows, tile_rows):
    t, d = h.shape
    tm = _pick_tile(t, 2048, SUBLANES)
    return pl.pallas_call(
        _dispatch_kernel,
        grid=(t // tm,),
        in_specs=[pl.BlockSpec(zero_tiles.shape, lambda i: (0,), memory_space=pltpu.SMEM)]
        + [pl.BlockSpec((tm,), lambda i: (i,), memory_space=pltpu.SMEM)] * TOP_K
        + [pl.BlockSpec((tm, d), lambda i: (i, 0))],
        out_specs=pl.BlockSpec(memory_space=pl.ANY),
        out_shape=jax.ShapeDtypeStruct((n_rows, d), F32),
        scratch_shapes=[pltpu.VMEM((tile_rows, d), F32), pltpu.SemaphoreType.DMA(())],
        compiler_params=_params(("arbitrary",)),
        name="moe_dispatch",
    )(zero_tiles, *slots, h)


def _grouped_ffn_kernel(te_ref, nu_ref, x_ref, g_ref, wg_hbm, wu_hbm, wd_hbm, y_ref,
                        wg_ref, wu_ref, wd_ref, stage_in_ref, stage_out_ref, sems):
    i = pl.program_id(0)
    used = i < nu_ref[0]
    expert = te_ref[i]

    @pl.when(jnp.logical_not(used))
    def _():
        y_ref[...] = jnp.zeros_like(y_ref)

    @pl.when(used & ((i == 0) | (te_ref[jnp.maximum(i - 1, 0)] != expert)))
    def _():
        chunks = []
        for src, stage, dst in ((wg_hbm, stage_in_ref, wg_ref), (wu_hbm, stage_in_ref, wu_ref),
                                (wd_hbm, stage_out_ref, wd_ref)):
            rows = stage.shape[1]
            for c in range(dst.shape[0] // rows):
                chunks.append((src.at[expert, pl.ds(c * rows, rows)], stage, dst.at[pl.ds(c * rows, rows)]))

        slots = sems.shape[0]

        def copy(n):
            src, stage, _ = chunks[n]
            return pltpu.make_async_copy(src, stage.at[n % slots], sems.at[n % slots])

        for n in range(slots - 1):
            copy(n).start()
        for n in range(len(chunks)):
            copy(n).wait()
            ahead = n + slots - 1
            if ahead < len(chunks):
                copy(ahead).start()
            _, stage, dst = chunks[n]
            dst[...] = stage[n % slots].astype(BF16)

    @pl.when(used)
    def _():
        xn = _rms(x_ref[...], g_ref[...]).astype(BF16)
        act = _silu(_dot(xn, wg_ref[...])) * _dot(xn, wu_ref[...])
        y_ref[...] = _dot(act.astype(BF16), wd_ref[...])


def _grouped_ffn(xs, g, w_gate, w_up, w_down, tile_expert, n_used, *, tile_rows):
    n_rows, d = xs.shape
    ff = w_gate.shape[2]
    assert d % WEIGHT_CHUNKS == 0 and ff % WEIGHT_CHUNKS == 0

    def x_map(i, te, nu):
        return (jnp.minimum(i, nu[0] - 1), 0)

    return pl.pallas_call(
        _grouped_ffn_kernel,
        grid_spec=pltpu.PrefetchScalarGridSpec(
            num_scalar_prefetch=2,
            grid=(n_rows // tile_rows,),
            in_specs=[
                pl.BlockSpec((tile_rows, d), x_map),
                pl.BlockSpec((1, d), lambda i, te, nu: (0, 0)),
                pl.BlockSpec(memory_space=pl.ANY),
                pl.BlockSpec(memory_space=pl.ANY),
                pl.BlockSpec(memory_space=pl.ANY),
            ],
            out_specs=pl.BlockSpec((tile_rows, d), lambda i, te, nu: (i, 0)),
            scratch_shapes=[
                pltpu.VMEM((d, ff), BF16), pltpu.VMEM((d, ff), BF16), pltpu.VMEM((ff, d), BF16),
                pltpu.VMEM((WEIGHT_SLOTS, d // WEIGHT_CHUNKS, ff), F32),
                pltpu.VMEM((WEIGHT_SLOTS, ff // WEIGHT_CHUNKS, d), F32),
                pltpu.SemaphoreType.DMA((WEIGHT_SLOTS,)),
            ],
        ),
        out_shape=jax.ShapeDtypeStruct((n_rows, d), F32),
        compiler_params=_params(("arbitrary",)),
        name="moe_grouped_ffn",
    )(tile_expert, n_used, xs, g, w_gate, w_up, w_down)


def _combine_kernel(*refs):
    slot_refs, (h_ref, route_ref, gf_ref, y_hbm, o_ref, ybuf_ref, sem) = refs[:TOP_K], refs[TOP_K:]
    tm = h_ref.shape[0]

    @pl.loop(0, tm // ROW_GROUP)
    def _(g):
        base = pl.multiple_of(g * ROW_GROUP, ROW_GROUP)
        for j in range(TOP_K):
            rows = ybuf_ref.at[j, pl.ds(base, ROW_GROUP)]
            for u in range(ROW_GROUP):
                _row_copy(y_hbm, slot_refs[j][base + u], rows, u, sem).start(priority=j % 2)

    for j in range(TOP_K):
        pltpu.make_async_copy(y_hbm.at[pl.ds(0, tm)], ybuf_ref.at[j], sem).wait()

    route = route_ref[...]
    out = h_ref[...]
    for j in range(TOP_K):
        out = out + route[:, TOP_K + j:TOP_K + j + 1] * ybuf_ref[j]
    o_ref[...] = _rms(out, gf_ref[...])


def _combine(h, route, slots, y, final_g):
    t, d = h.shape
    tm = _pick_tile(t, 1024, SUBLANES)
    row = lambda n: pl.BlockSpec((tm, n), lambda i: (i, 0))
    return pl.pallas_call(
        _combine_kernel,
        grid=(t // tm,),
        in_specs=[pl.BlockSpec((tm,), lambda i: (i,), memory_space=pltpu.SMEM)] * TOP_K
        + [row(d), row(LANES), pl.BlockSpec((1, d), lambda i: (0, 0)), pl.BlockSpec(memory_space=pl.ANY)],
        out_specs=row(d),
        out_shape=jax.ShapeDtypeStruct((t, d), F32),
        scratch_shapes=[pltpu.VMEM((TOP_K, tm, d), F32), pltpu.SemaphoreType.DMA(())],
        compiler_params=_params(("arbitrary",)),
        name="moe_combine",
    )(*slots, h, route, final_g, y)


def _pad_lanes(a, offset=0, fill=0.0):
    n = a.shape[-1]
    pad = [(0, 0)] * (a.ndim - 1) + [(offset, LANES - offset - n)]
    return jnp.pad(a, pad, constant_values=fill)


def kernel(x, norm_mix_g, norm_ffn_g, pool_w_in, pool_w_group, pool_scale, dn_w_in, dn_conv_w, dn_a_log, dn_dt_bias, dn_norm_g, dn_w_out, ffn_w_gate, ffn_w_up, ffn_w_down, moe_router_w, moe_router_b, moe_w_gate, moe_w_up, moe_w_down, final_norm_g):
    b, s, d = x.shape
    t = b * s
    kd = dn_w_out.shape[1]
    heads = kd // DN_HEAD_DIM
    n_experts = moe_router_w.shape[-1]
    assert 2 * heads <= RECORD_ROWS and n_experts <= LANES and 2 * TOP_K <= RECORD_ROWS
    vec = lambda a: a.reshape(1, -1).astype(F32)

    h = x.reshape(t, d)

    h = _layer0(h, s, vec(norm_mix_g[0]), pool_w_in[0].astype(BF16), pool_w_group[0].astype(BF16),
                vec(pool_scale[0]), vec(norm_ffn_g[0]), ffn_w_gate[0].astype(BF16), ffn_w_up[0].astype(BF16),
                ffn_w_down[0].astype(BF16))

    w_in = dn_w_in[0]
    w_small = _pad_lanes(w_in[:, 4 * kd:]).astype(BF16)
    q, k, v, z, bg, bg_t = _dn_in_proj(
        h, s, vec(norm_mix_g[1]), w_in[:, :3 * kd].astype(BF16), w_in[:, 3 * kd:4 * kd].astype(BF16), w_small,
        dn_conv_w[0].astype(F32), _pad_lanes(vec(dn_a_log[0]), heads), _pad_lanes(vec(dn_dt_bias[0]), heads))
    g_rows = bg_t[heads:2 * heads].reshape(heads, b, s // DN_CHUNK, DN_CHUNK).transpose(1, 2, 0, 3)
    shape3 = lambda a: a.reshape(b, s, a.shape[-1])
    o = _delta_rule(shape3(q), shape3(k), shape3(v), shape3(z), shape3(bg), g_rows, vec(dn_norm_g[0]),
                    seqs_per_step=4 if b % 4 == 0 else (2 if b % 2 == 0 else 1))
    h, route, route_t = _out_proj_router(
        h, o.reshape(t, kd), dn_w_out[0].astype(BF16), vec(norm_ffn_g[1]),
        _pad_lanes(moe_router_w[0].astype(F32)), _pad_lanes(vec(moe_router_b[0]), fill=-jnp.inf))

    tile_rows = _pick_tile(TOP_K * t, MOE_TILE_ROWS, SUBLANES)
    slots, tile_expert, n_used, zero_tiles = _routing_tables(route_t, n_experts, tile_rows)
    xs = _dispatch(h, slots, zero_tiles, tile_expert.shape[0] * tile_rows, tile_rows)
    y = _grouped_ffn(xs, vec(norm_ffn_g[1]), moe_w_gate[0].astype(F32), moe_w_up[0].astype(F32),
                     moe_w_down[0].astype(F32), tile_expert, n_used, tile_rows=tile_rows)
    out = _combine(h, route, slots, y, vec(final_norm_g))
    return out.reshape(b, s, d)
```

```python
import functools

import jax
import jax.numpy as jnp
from jax import lax
from jax.experimental import pallas as pl
from jax.experimental.pallas import tpu as pltpu

F32 = jnp.float32
BF16 = jnp.bfloat16

RMS_EPS = 1e-6
POOL_WINDOWS = (2, 4, 8, 16)
DN_HEAD_DIM = 128
DN_CHUNK = 64
TOP_K = 2

LANES = 128
SUBLANES = 8
VMEM_LIMIT_BYTES = 56 * 1024 * 1024

POOL_HALO = 16
CONV_HALO = SUBLANES

MOE_TILE_ROWS = 512
ROW_GROUP = SUBLANES
WEIGHT_CHUNKS = 16
WEIGHT_SLOTS = 6
RECORD_ROWS = 16


def _dot(a, b):
    return jnp.dot(a, b, preferred_element_type=F32)


def _dot_f32(a, b):
    return jnp.dot(a, b, preferred_element_type=F32, precision=lax.Precision.HIGHEST)


def _dot_nt(a, b):
    return lax.dot_general(a, b, (((1,), (1,)), ((), ())), preferred_element_type=F32)


def _dot_tn(a, b):
    return lax.dot_general(a, b, (((0,), (0,)), ((), ())), preferred_element_type=F32)


def _rms(x, g):
    return x * lax.rsqrt(jnp.mean(x * x, axis=-1, keepdims=True) + RMS_EPS) * g


def _silu(x):
    half = 0.5 * x
    return half + half * jnp.tanh(half)


def _pick_tile(n, target, quantum):
    if n <= target:
        return n
    best = None
    for t in range(quantum, target + 1, quantum):
        if n % t == 0:
            best = t
    assert best is not None, (n, target, quantum)
    return best


def _params(semantics):
    return pltpu.CompilerParams(dimension_semantics=semantics, vmem_limit_bytes=VMEM_LIMIT_BYTES)


def _resident(shape):
    nd = len(shape)
    return pl.BlockSpec(shape, lambda *_: (0,) * nd, pipeline_mode=pl.Buffered(1))


def _layer0_kernel(x_ref, xh_ref, gm_ref, win_ref, wgrp_ref, sc_ref, gf_ref, wg_ref, wu_ref, wd_ref,
                   o_ref, ext_ref, h_ref, *, tiles_per_seq):
    tm = x_ref.shape[0]
    j = pl.program_id(0) % tiles_per_seq
    x = x_ref[...]
    gm = gm_ref[...]
    win = win_ref[...]
    u = _dot(_rms(x, gm).astype(BF16), win)
    uh = _dot(_rms(xh_ref[...], gm).astype(BF16), win)
    ext_ref[0:POOL_HALO, :] = jnp.where(j == 0, 0.0, uh)
    ext_ref[POOL_HALO:, :] = u
    pos = (j * tm + lax.broadcasted_iota(jnp.int32, (tm, 1), 0) + 1).astype(F32)
    c = u.shape[1] // len(POOL_WINDOWS)
    for gi, w in enumerate(POOL_WINDOWS):
        sl = slice(gi * c, (gi + 1) * c)
        s = ext_ref[:, sl]
        sh = 1
        while sh < w:
            s = s + pltpu.roll(s, sh, 0)
            sh *= 2
        m = s[POOL_HALO:] / jnp.minimum(pos, float(w)) - u[:, sl]
        y = _dot(m.astype(BF16), wgrp_ref[gi])
        h_ref[:, sl] = x[:, sl] + y * sc_ref[:, sl]
    h = h_ref[...]
    hn = _rms(h, gf_ref[...]).astype(BF16)
    act = _silu(_dot(hn, wg_ref[...])) * _dot(hn, wu_ref[...])
    o_ref[...] = h + _dot(act.astype(BF16), wd_ref[...])


def _layer0(x2d, seq, g_mix, w_in, w_group, scale, g_ffn, w_gate, w_up, w_down):
    t, d = x2d.shape
    assert all(w & (w - 1) == 0 and w <= POOL_HALO for w in POOL_WINDOWS)
    tm = _pick_tile(seq, 512, POOL_HALO)
    assert tm % POOL_HALO == 0
    halo_blocks = tm // POOL_HALO
    row = pl.BlockSpec((tm, d), lambda i: (i, 0))
    return pl.pallas_call(
        functools.partial(_layer0_kernel, tiles_per_seq=seq // tm),
        grid=(t // tm,),
        in_specs=[
            row,
            pl.BlockSpec((POOL_HALO, d), lambda i: (jnp.maximum(i * halo_blocks - 1, 0), 0)),
            _resident((1, d)), _resident(w_in.shape), _resident(w_group.shape), _resident((1, d)),
            _resident((1, d)), _resident(w_gate.shape), _resident(w_up.shape), _resident(w_down.shape),
        ],
        out_specs=row,
        out_shape=jax.ShapeDtypeStruct((t, d), F32),
        scratch_shapes=[pltpu.VMEM((tm + POOL_HALO, d), F32), pltpu.VMEM((tm, d), F32)],
        compiler_params=_params(("parallel",)),
        name="layer0_pool_ffn",
    )(x2d, x2d, g_mix, w_in, w_group, scale, g_ffn, w_gate, w_up, w_down)


def _dn_in_kernel(h_ref, hh_ref, g_ref, wqkv_ref, wz_ref, wsm_ref, cw_ref, alog_ref, dtb_ref,
                  q_ref, k_ref, v_ref, z_ref, bg_ref, bgt_ref, ext_ref, *, tiles_per_seq):
    j = pl.program_id(0) % tiles_per_seq
    g = g_ref[...]
    hn = _rms(h_ref[...], g).astype(BF16)
    hnh = jnp.where(j == 0, 0.0, _rms(hh_ref[...], g)).astype(BF16)
    kd = q_ref.shape[1]
    heads = kd // DN_HEAD_DIM
    taps = cw_ref.shape[0]
    for part, out_ref in enumerate((q_ref, k_ref, v_ref)):
        cols = slice(part * kd, (part + 1) * kd)
        w = wqkv_ref[:, cols]
        ext_ref[0:CONV_HALO, :] = _dot(hnh, w)
        ext_ref[CONV_HALO:, :] = _dot(hn, w)
        p = ext_ref[...]
        cw = cw_ref[:, cols]
        y = p * cw[taps - 1:taps, :]
        for s in range(1, taps):
            y = y + pltpu.roll(p, s, 0) * cw[taps - 1 - s:taps - s, :]
        y = _silu(y[CONV_HALO:])
        if part == 2:
            out_ref[...] = y
            continue
        for h in range(heads):
            hs = slice(h * DN_HEAD_DIM, (h + 1) * DN_HEAD_DIM)
            yh = y[:, hs]
            yh = yh * lax.rsqrt(jnp.sum(yh * yh, axis=-1, keepdims=True) + RMS_EPS)
            if part == 0:
                yh = yh * (DN_HEAD_DIM ** -0.5)
            out_ref[:, hs] = yh
    z_ref[...] = _dot(hn, wz_ref[...])
    sm = _dot(hn, wsm_ref[...])
    xs = sm + dtb_ref[...]
    softplus = jnp.maximum(xs, 0.0) + jnp.log1p(jnp.exp(-jnp.abs(xs)))
    lane = lax.broadcasted_iota(jnp.int32, sm.shape, 1)
    bg = jnp.where(lane < heads, jax.nn.sigmoid(sm), -jnp.exp(alog_ref[...]) * softplus)
    bg_ref[...] = bg
    bgt_ref[...] = bg.T[:bgt_ref.shape[0]]


def _dn_in_proj(h, seq, g, w_qkv, w_z, w_small, conv_w, a_log_pad, dt_bias_pad):
    t, d = h.shape
    kd = w_z.shape[1]
    assert conv_w.shape[0] - 1 <= CONV_HALO
    tm = _pick_tile(seq, 512, CONV_HALO)
    halo_blocks = tm // CONV_HALO
    row = lambda n: pl.BlockSpec((tm, n), lambda i: (i, 0))
    return pl.pallas_call(
        functools.partial(_dn_in_kernel, tiles_per_seq=seq // tm),
        grid=(t // tm,),
        in_specs=[
            row(d),
            pl.BlockSpec((CONV_HALO, d), lambda i: (jnp.maximum(i * halo_blocks - 1, 0), 0)),
            _resident((1, d)),
            _resident(w_qkv.shape),
            _resident(w_z.shape),
            _resident(w_small.shape),
            _resident(conv_w.shape),
            _resident((1, LANES)),
            _resident((1, LANES)),
        ],
        out_specs=[row(kd), row(kd), row(kd), row(kd), row(LANES),
                   pl.BlockSpec((RECORD_ROWS, tm), lambda i: (0, i))],
        out_shape=[jax.ShapeDtypeStruct((t, kd), F32)] * 4 + [jax.ShapeDtypeStruct((t, LANES), F32),
                                                              jax.ShapeDtypeStruct((RECORD_ROWS, t), F32)],
        scratch_shapes=[pltpu.VMEM((tm + CONV_HALO, kd), F32)],
        compiler_params=_params(("parallel",)),
        name="deltanet_in_proj",
    )(h, h, g, w_qkv, w_z, w_small, conv_w, a_log_pad, dt_bias_pad)


def _delta_kernel(q_ref, k_ref, v_ref, z_ref, bg_ref, gt_ref, ng_ref, o_ref, state_ref):
    nb, c, kd = q_ref.shape
    heads = kd // DN_HEAD_DIM
    pairs = [(b, h) for b in range(nb) for h in range(heads)]

    @pl.when(pl.program_id(1) == 0)
    def _():
        state_ref[...] = jnp.zeros_like(state_ref)

    def head_cols(h):
        return slice(h * DN_HEAD_DIM, (h + 1) * DN_HEAD_DIM)

    row = lax.broadcasted_iota(jnp.int32, (c, c), 0)
    col = lax.broadcasted_iota(jnp.int32, (c, c), 1)
    causal = row >= col
    strict = row > col
    diag = row == col
    lower_ones = causal.astype(F32)
    upper_ones = (row <= col).astype(F32)
    bg = [bg_ref[b] for b in range(nb)]
    g_cum_col = [_dot_f32(lower_ones, bg[b]) for b in range(nb)]
    g_cum_row = [_dot_f32(gt_ref[b, 0], upper_ones) for b in range(nb)]

    q = [q_ref[b, :, head_cols(h)] for b, h in pairs]
    k = [k_ref[b, :, head_cols(h)] for b, h in pairs]
    beta = [bg[b][:, h:h + 1] for b, h in pairs]
    gc = [g_cum_col[b][:, heads + h:heads + h + 1] for b, h in pairs]
    gr = [g_cum_row[b][h:h + 1, :] for b, h in pairs]
    n = len(pairs)
    rng = range(n)
    decay = [jnp.where(causal, jnp.exp(jnp.where(causal, gc[i] - gr[i], 0.0)), 0.0) for i in rng]
    k_beta = [k[i] * beta[i] for i in rng]
    scores = [_dot_nt(jnp.concatenate([k_beta[i], q[i]], axis=0).astype(BF16), k[i].astype(BF16))
              for i in rng]
    qk = [(scores[i][c:] * decay[i]).astype(BF16) for i in rng]
    x = [jnp.where(strict, -(scores[i][:c] * decay[i]), 0.0) for i in rng]
    inv = [jnp.where(diag, 1.0, x[i]) for i in rng]
    p = 1
    while 2 * p < c:
        x16 = [x[i].astype(BF16) for i in rng]
        x = [_dot(x16[i], x16[i]) for i in rng]
        inv = [inv[i] + _dot(inv[i].astype(BF16), x[i].astype(BF16)) for i in rng]
        p *= 2
    e_gc = [jnp.exp(gc[i]) for i in rng]
    rhs = [jnp.concatenate([v_ref[b, :, head_cols(h)] * beta[i], k_beta[i] * e_gc[i]], axis=1).astype(BF16)
           for i, (b, h) in enumerate(pairs)]
    uw = [_dot(inv[i].astype(BF16), rhs[i]) for i in rng]
    state = [state_ref[i] for i in rng]
    ws_qs = [_dot(jnp.concatenate([uw[i][:, DN_HEAD_DIM:], q[i] * e_gc[i]], axis=0).astype(BF16),
                  state[i].astype(BF16)) for i in rng]
    v_new = [(uw[i][:, :DN_HEAD_DIM] - ws_qs[i][:c]).astype(BF16) for i in rng]
    o = [ws_qs[i][c:] + _dot(qk[i], v_new[i]) for i in rng]
    g_last = [gc[i][c - 1:c, :] for i in rng]
    k_dec = [(k[i] * jnp.exp(g_last[i] - gc[i])).astype(BF16) for i in rng]
    for i in rng:
        state_ref[i] = state[i] * jnp.exp(g_last[i]) + _dot_tn(k_dec[i], v_new[i])
    ng = ng_ref[...]
    for i, (b, h) in enumerate(pairs):
        on = o[i] * lax.rsqrt(jnp.mean(o[i] * o[i], axis=-1, keepdims=True) + RMS_EPS) * ng
        o_ref[b, :, head_cols(h)] = (on * _silu(z_ref[b, :, head_cols(h)])).astype(BF16)


def _delta_rule(q, k, v, z, bg, g_rows, norm_g, *, seqs_per_step):
    b, s, kd = q.shape
    heads = kd // DN_HEAD_DIM
    c = DN_CHUNK
    nb = seqs_per_step
    assert s % c == 0 and c & (c - 1) == 0 and b % nb == 0
    blk = lambda n: pl.BlockSpec((nb, c, n), lambda i, j: (i, j, 0))
    return pl.pallas_call(
        _delta_kernel,
        grid=(b // nb, s // c),
        in_specs=[blk(kd), blk(kd), blk(kd), blk(kd), blk(LANES),
                  pl.BlockSpec((nb, 1, heads, c), lambda i, j: (i, j, 0, 0)),
                  pl.BlockSpec((1, DN_HEAD_DIM), lambda i, j: (0, 0))],
        out_specs=blk(kd),
        out_shape=jax.ShapeDtypeStruct((b, s, kd), BF16),
        scratch_shapes=[pltpu.VMEM((nb * heads, DN_HEAD_DIM, DN_HEAD_DIM), F32)],
        compiler_params=_params(("parallel", "arbitrary")),
        name="delta_rule",
    )(q, k, v, z, bg, g_rows, norm_g)


def _split_bf16(a):
    hi = a.astype(BF16)
    return hi, (a - hi.astype(F32)).astype(BF16)


def _out_router_kernel(h_ref, o_ref, wo_ref, g_ref, rw_hi_ref, rw_lo_ref, rb_ref, h_out_ref, route_ref, route_t_ref):
    h = h_ref[...] + _dot(o_ref[...], wo_ref[...])
    h_out_ref[...] = h
    hn_hi, hn_lo = _split_bf16(_rms(h, g_ref[...]))
    rw_hi = rw_hi_ref[...]
    logits = _dot(hn_hi, rw_hi) + (_dot(hn_lo, rw_hi) + _dot(hn_hi, rw_lo_ref[...]))
    logits = logits + rb_ref[...]
    lane = lax.broadcasted_iota(jnp.int32, logits.shape, 1)
    remaining = logits
    tops, picks = [], []
    for _ in range(TOP_K):
        top = jnp.max(remaining, axis=1, keepdims=True)
        pick = jnp.min(jnp.where(remaining == top, lane, LANES), axis=1, keepdims=True)
        tops.append(top)
        picks.append(pick)
        remaining = jnp.where(lane == pick, -jnp.inf, remaining)
    exps = [jnp.exp(top - tops[0]) for top in tops]
    denom = sum(exps)
    route = jnp.zeros_like(logits)
    for j, (pick, ex) in enumerate(zip(picks, exps)):
        route = jnp.where(lane == j, pick.astype(F32), route)
        route = jnp.where(lane == TOP_K + j, ex / denom, route)
    route_ref[...] = route
    route_t_ref[...] = route.T[:route_t_ref.shape[0]]


def _out_proj_router(h, o, w_out, g, router_w_pad, router_b_pad):
    rw_hi, rw_lo = _split_bf16(router_w_pad)
    t, d = h.shape
    tm = _pick_tile(t, 512, SUBLANES)
    row = lambda n: pl.BlockSpec((tm, n), lambda i: (i, 0))
    return pl.pallas_call(
        _out_router_kernel,
        grid=(t // tm,),
        in_specs=[row(d), row(o.shape[1]), _resident(w_out.shape), _resident((1, d)),
                  _resident(router_w_pad.shape), _resident(router_w_pad.shape), _resident((1, LANES))],
        out_specs=[row(d), row(LANES), pl.BlockSpec((RECORD_ROWS, tm), lambda i: (0, i))],
        out_shape=[jax.ShapeDtypeStruct((t, d), F32), jax.ShapeDtypeStruct((t, LANES), F32),
                   jax.ShapeDtypeStruct((RECORD_ROWS, t), F32)],
        compiler_params=_params(("parallel",)),
        name="out_proj_router",
    )(h, o, w_out, g, rw_hi, rw_lo, router_b_pad)


def _routing_tables(route_t, n_experts, tile_rows):
    t = route_t.shape[1]
    assert (TOP_K * t) % tile_rows == 0
    idx = route_t[:TOP_K].astype(jnp.int32)
    experts = jnp.arange(n_experts, dtype=jnp.int32)[:, None]
    hits = [idx[j][None, :] == experts for j in range(TOP_K)]
    member = functools.reduce(jnp.logical_or, hits).astype(jnp.int32)
    counts = member.sum(axis=1)
    rank = jnp.cumsum(member, axis=1) - member
    padded = (counts + tile_rows - 1) // tile_rows * tile_rows
    ends = jnp.cumsum(padded)
    starts = ends - padded
    row_of = starts[:, None] + rank
    slots = [jnp.sum(jnp.where(hit, row_of, 0), axis=0).astype(jnp.int32) for hit in hits]
    n_tiles = TOP_K * t // tile_rows + n_experts
    n_used = (ends[-1] // tile_rows).astype(jnp.int32)
    tile_start = jnp.arange(n_tiles, dtype=jnp.int32) * tile_rows
    tile_expert = jnp.sum(tile_start[:, None] >= ends[None, :], axis=1).astype(jnp.int32)
    tile_expert = jnp.where(jnp.arange(n_tiles) < n_used, tile_expert, tile_expert[n_used - 1])
    last_tile = jnp.where(padded > 0, ends // tile_rows - 1, -1)
    trailing = n_used + jnp.arange(n_experts, dtype=jnp.int32)
    trailing = jnp.where(trailing < n_tiles, trailing, -1)
    zero_tiles = jnp.concatenate([last_tile, trailing]).astype(jnp.int32)
    return slots, tile_expert, n_used.reshape(1), zero_tiles


def _row_copy(src_ref, src_row, dst_ref, dst_row, sem):
    return pltpu.make_async_copy(src_ref.at[pl.ds(src_row, 1)], dst_ref.at[pl.ds(dst_row, 1)], sem)


def _dispatch_kernel(zero_tiles_ref, *refs):
    slot_refs, (h_ref, xs_hbm, zeros_ref, sem) = refs[:TOP_K], refs[TOP_K:]
    tm = h_ref.shape[0]
    tile_rows = zeros_ref.shape[0]

    @pl.when(pl.program_id(0) == 0)
    def _():
        zeros_ref[...] = jnp.zeros_like(zeros_ref)

        def fill(j):
            row = pl.multiple_of(jnp.maximum(zero_tiles_ref[j], 0) * tile_rows, tile_rows)
            return pltpu.make_async_copy(zeros_ref, xs_hbm.at[pl.ds(row, tile_rows)], sem)

        for j in range(zero_tiles_ref.shape[0]):
            pl.when(zero_tiles_ref[j] >= 0)(fill(j).start)
        for j in range(zero_tiles_ref.shape[0]):
            pl.when(zero_tiles_ref[j] >= 0)(fill(j).wait)

    @pl.loop(0, tm // ROW_GROUP)
    def _(g):
        base = pl.multiple_of(g * ROW_GROUP, ROW_GROUP)
        rows = h_ref.at[pl.ds(base, ROW_GROUP)]
        for u in range(ROW_GROUP):
            for j in range(TOP_K):
                _row_copy(rows, u, xs_hbm, slot_refs[j][base + u], sem).start(priority=j % 2)

    for _ in range(TOP_K):
        pltpu.make_async_copy(h_ref, xs_hbm.at[pl.ds(0, tm)], sem).wait()


def _dispatch(h, slots, zero_tiles, n_rows, tile_rows):
    t, d = h.shape
    tm = _pick_tile(t, 1024, SUBLANES)
    return pl.pallas_call(
        _dispatch_kernel,
        grid=(t // tm,),
        in_specs=[pl.BlockSpec(zero_tiles.shape, lambda i: (0,), memory_space=pltpu.SMEM)]
        + [pl.BlockSpec((tm,), lambda i: (i,), memory_space=pltpu.SMEM)] * TOP_K
        + [pl.BlockSpec((tm, d), lambda i: (i, 0))],
        out_specs=pl.BlockSpec(memory_space=pl.ANY),
        out_shape=jax.ShapeDtypeStruct((n_rows, d), F32),
        scratch_shapes=[pltpu.VMEM((tile_rows, d), F32), pltpu.SemaphoreType.DMA(())],
        compiler_params=_params(("arbitrary",)),
        name="moe_dispatch",
    )(zero_tiles, *slots, h)


def _grouped_ffn_kernel(te_ref, nu_ref, x_ref, g_ref, wg_hbm, wu_hbm, wd_hbm, y_ref,
                        wg_ref, wu_ref, wd_ref, stage_in_ref, stage_out_ref, sems):
    i = pl.program_id(0)
    used = i < nu_ref[0]
    expert = te_ref[i]

    @pl.when(jnp.logical_not(used))
    def _():
        y_ref[...] = jnp.zeros_like(y_ref)

    @pl.when(used & ((i == 0) | (te_ref[jnp.maximum(i - 1, 0)] != expert)))
    def _():
        chunks = []
        for src, stage, dst in ((wg_hbm, stage_in_ref, wg_ref), (wu_hbm, stage_in_ref, wu_ref),
                                (wd_hbm, stage_out_ref, wd_ref)):
            rows = stage.shape[1]
            for c in range(dst.shape[0] // rows):
                chunks.append((src.at[expert, pl.ds(c * rows, rows)], stage, dst.at[pl.ds(c * rows, rows)]))

        slots = sems.shape[0]

        def copy(n):
            src, stage, _ = chunks[n]
            return pltpu.make_async_copy(src, stage.at[n % slots], sems.at[n % slots])

        for n in range(slots - 1):
            copy(n).start()
        for n in range(len(chunks)):
            copy(n).wait()
            ahead = n + slots - 1
            if ahead < len(chunks):
                copy(ahead).start()
            _, stage, dst = chunks[n]
            dst[...] = stage[n % slots].astype(BF16)

    @pl.when(used)
    def _():
        xn = _rms(x_ref[...], g_ref[...]).astype(BF16)
        act = _silu(_dot(xn, wg_ref[...])) * _dot(xn, wu_ref[...])
        y_ref[...] = _dot(act.astype(BF16), wd_ref[...])


def _grouped_ffn(xs, g, w_gate, w_up, w_down, tile_expert, n_used, *, tile_rows):
    n_rows, d = xs.shape
    ff = w_gate.shape[2]
    assert d % WEIGHT_CHUNKS == 0 and ff % WEIGHT_CHUNKS == 0

    def x_map(i, te, nu):
        return (jnp.minimum(i, nu[0] - 1), 0)

    return pl.pallas_call(
        _grouped_ffn_kernel,
        grid_spec=pltpu.PrefetchScalarGridSpec(
            num_scalar_prefetch=2,
            grid=(n_rows // tile_rows,),
            in_specs=[
                pl.BlockSpec((tile_rows, d), x_map),
                pl.BlockSpec((1, d), lambda i, te, nu: (0, 0)),
                pl.BlockSpec(memory_space=pl.ANY),
                pl.BlockSpec(memory_space=pl.ANY),
                pl.BlockSpec(memory_space=pl.ANY),
            ],
            out_specs=pl.BlockSpec((tile_rows, d), lambda i, te, nu: (i, 0)),
            scratch_shapes=[
                pltpu.VMEM((d, ff), BF16), pltpu.VMEM((d, ff), BF16), pltpu.VMEM((ff, d), BF16),
                pltpu.VMEM((WEIGHT_SLOTS, d // WEIGHT_CHUNKS, ff), F32),
                pltpu.VMEM((WEIGHT_SLOTS, ff // WEIGHT_CHUNKS, d), F32),
                pltpu.SemaphoreType.DMA((WEIGHT_SLOTS,)),
            ],
        ),
        out_shape=jax.ShapeDtypeStruct((n_rows, d), F32),
        compiler_params=_params(("arbitrary",)),
        name="moe_grouped_ffn",
    )(tile_expert, n_used, xs, g, w_gate, w_up, w_down)


def _combine_kernel(*refs):
    slot_refs, (h_ref, route_ref, gf_ref, y_hbm, o_ref, ybuf_ref, sem) = refs[:TOP_K], refs[TOP_K:]
    tm = h_ref.shape[0]

    @pl.loop(0, tm // ROW_GROUP)
    def _(g):
        base = pl.multiple_of(g * ROW_GROUP, ROW_GROUP)
        for j in range(TOP_K):
            rows = ybuf_ref.at[j, pl.ds(base, ROW_GROUP)]
            for u in range(ROW_GROUP):
                _row_copy(y_hbm, slot_refs[j][base + u], rows, u, sem).start(priority=j % 2)

    for j in range(TOP_K):
        pltpu.make_async_copy(y_hbm.at[pl.ds(0, tm)], ybuf_ref.at[j], sem).wait()

    route = route_ref[...]
    out = h_ref[...]
    for j in range(TOP_K):
        out = out + route[:, TOP_K + j:TOP_K + j + 1] * ybuf_ref[j]
    o_ref[...] = _rms(out, gf_ref[...])


def _combine(h, route, slots, y, final_g):
    t, d = h.shape
    tm = _pick_tile(t, 512, SUBLANES)
    row = lambda n: pl.BlockSpec((tm, n), lambda i: (i, 0))
    return pl.pallas_call(
        _combine_kernel,
        grid=(t // tm,),
        in_specs=[pl.BlockSpec((tm,), lambda i: (i,), memory_space=pltpu.SMEM)] * TOP_K
        + [row(d), row(LANES), pl.BlockSpec((1, d), lambda i: (0, 0)), pl.BlockSpec(memory_space=pl.ANY)],
        out_specs=row(d),
        out_shape=jax.ShapeDtypeStruct((t, d), F32),
        scratch_shapes=[pltpu.VMEM((TOP_K, tm, d), F32), pltpu.SemaphoreType.DMA(())],
        compiler_params=_params(("arbitrary",)),
        name="moe_combine",
    )(*slots, h, route, final_g, y)


def _pad_lanes(a, offset=0, fill=0.0):
    n = a.shape[-1]
    pad = [(0, 0)] * (a.ndim - 1) + [(offset, LANES - offset - n)]
    return jnp.pad(a, pad, constant_values=fill)


def kernel(x, norm_mix_g, norm_ffn_g, pool_w_in, pool_w_group, pool_scale, dn_w_in, dn_conv_w, dn_a_log, dn_dt_bias, dn_norm_g, dn_w_out, ffn_w_gate, ffn_w_up, ffn_w_down, moe_router_w, moe_router_b, moe_w_gate, moe_w_up, moe_w_down, final_norm_g):
    b, s, d = x.shape
    t = b * s
    kd = dn_w_out.shape[1]
    heads = kd // DN_HEAD_DIM
    n_experts = moe_router_w.shape[-1]
    assert 2 * heads <= RECORD_ROWS and n_experts <= LANES and 2 * TOP_K <= RECORD_ROWS
    vec = lambda a: a.reshape(1, -1).astype(F32)

    h = x.reshape(t, d)

    h = _layer0(h, s, vec(norm_mix_g[0]), pool_w_in[0].astype(BF16), pool_w_group[0].astype(BF16),
                vec(pool_scale[0]), vec(norm_ffn_g[0]), ffn_w_gate[0].astype(BF16), ffn_w_up[0].astype(BF16),
                ffn_w_down[0].astype(BF16))

    w_in = dn_w_in[0]
    w_small = _pad_lanes(w_in[:, 4 * kd:]).astype(BF16)
    q, k, v, z, bg, bg_t = _dn_in_proj(
        h, s, vec(norm_mix_g[1]), w_in[:, :3 * kd].astype(BF16), w_in[:, 3 * kd:4 * kd].astype(BF16), w_small,
        dn_conv_w[0].astype(F32), _pad_lanes(vec(dn_a_log[0]), heads), _pad_lanes(vec(dn_dt_bias[0]), heads))
    g_rows = bg_t[heads:2 * heads].reshape(heads, b, s // DN_CHUNK, DN_CHUNK).transpose(1, 2, 0, 3)
    shape3 = lambda a: a.reshape(b, s, a.shape[-1])
    o = _delta_rule(shape3(q), shape3(k), shape3(v), shape3(z), shape3(bg), g_rows, vec(dn_norm_g[0]),
                    seqs_per_step=4 if b % 4 == 0 else (2 if b % 2 == 0 else 1))
    h, route, route_t = _out_proj_router(
        h, o.reshape(t, kd), dn_w_out[0].astype(BF16), vec(norm_ffn_g[1]),
        _pad_lanes(moe_router_w[0].astype(F32)), _pad_lanes(vec(moe_router_b[0]), fill=-jnp.inf))

    tile_rows = _pick_tile(TOP_K * t, MOE_TILE_ROWS, SUBLANES)
    slots, tile_expert, n_used, zero_tiles = _routing_tables(route_t, n_experts, tile_rows)
    xs = _dispatch(h, slots, zero_tiles, tile_expert.shape[0] * tile_rows, tile_rows)
    y = _grouped_ffn(xs, vec(norm_ffn_g[1]), moe_w_gate[0].astype(F32), moe_w_up[0].astype(F32),
                     moe_w_down[0].astype(F32), tile_expert, n_used, tile_rows=tile_rows)
    out = _combine(h, route, slots, y, vec(final_norm_g))
    return out.reshape(b, s, d)
```
